```python
import math
import jax, jax.numpy as jnp
from jax import lax
import numpy as np

D_MODEL = 1024
BATCH = 1
SEQ = 16384
DEPTH = 2

GRID_W = 64
ATT_HEADS = 8
ATT_KV_HEADS = 2
ATT_HEAD_DIM = 64
ATT_WIDTH = ATT_HEADS * ATT_HEAD_DIM
ATT_KV_WIDTH = ATT_KV_HEADS * ATT_HEAD_DIM
ROPE_AXIS_DIM = ATT_HEAD_DIM // 2
ROPE_THETA = 10000.0
Q_BLOCK = 128
HG_HEADS = 4
HG_F_DIM = 128
HG_I_DIM = 128
HG_F_WIDTH = HG_HEADS * HG_F_DIM
HG_WIDTH = HG_HEADS * HG_I_DIM
HG_CHUNK = 64
MIN_FORGET = 1e-6
N_BRANCHES = 2
N_EXPERTS = 16
EC_CAPACITY_FACTOR = 2
EXPERT_FF = 2048
NORM_EPS = 1e-6

IN_SPLITS = (ATT_WIDTH, ATT_KV_WIDTH, ATT_KV_WIDTH,
             HG_F_WIDTH, HG_F_WIDTH, HG_F_WIDTH,
             HG_WIDTH, HG_WIDTH,
             N_BRANCHES * D_MODEL)
IN_WIDTH = sum(IN_SPLITS)

kernel_name = "hybrid_gqa_hgrn2_ecmoe_encoder"


def rms_norm(x, w):
    xf = x.astype(jnp.float32)
    y = xf * lax.rsqrt(jnp.mean(xf * xf, axis=-1, keepdims=True) + NORM_EPS)
    return (y * w.astype(jnp.float32)).astype(x.dtype)


def axial_rope_tables(seq_len):
    rows = seq_len // GRID_W
    row_id = jnp.repeat(jnp.arange(rows, dtype=jnp.float32), GRID_W)
    col_id = jnp.tile(jnp.arange(GRID_W, dtype=jnp.float32), rows)
    inv_freq = ROPE_THETA ** (-jnp.arange(0, ROPE_AXIS_DIM, 2, dtype=jnp.float32) / ROPE_AXIS_DIM)
    ang = jnp.stack([row_id[:, None] * inv_freq, col_id[:, None] * inv_freq], axis=1)
    return jnp.cos(ang), jnp.sin(ang)


def apply_axial_rope(x, cos, sin):
    B, S, H, hd = x.shape
    xr = x.reshape(B, S, H, 2, 2, ROPE_AXIS_DIM // 2)
    x1, x2 = xr[..., 0, :], xr[..., 1, :]
    cs, sn = cos[None, :, None], sin[None, :, None]
    out = jnp.stack([x1 * cs - x2 * sn, x2 * cs + x1 * sn], axis=-2)
    return out.reshape(B, S, H, hd)


def blocked_bidirectional_gqa(q, k, v):
    B, S, _, _ = q.shape
    G = ATT_HEADS // ATT_KV_HEADS
    n_blk = S // Q_BLOCK
    qb = q.reshape(B, n_blk, Q_BLOCK, ATT_KV_HEADS, G, ATT_HEAD_DIM).transpose(1, 0, 3, 4, 2, 5)
    kt = k.transpose(0, 2, 1, 3)
    vt = v.transpose(0, 2, 1, 3)
    scale = ATT_HEAD_DIM ** -0.5

    def one_block(qblk):
        s = jnp.einsum('bhgqd,bhkd->bhgqk', qblk, kt) * scale
        p = jax.nn.softmax(s, axis=-1)
        return jnp.einsum('bhgqk,bhkd->bhgqd', p, vt)

    o = lax.map(one_block, qb)
    return o.transpose(1, 0, 4, 2, 3, 5).reshape(B, S, ATT_WIDTH)


def gla_chunk_scan(q, k, v, log_f):
    B, S, H, Dk = q.shape
    Dv = v.shape[-1]
    n = S // HG_CHUNK

    def to_chunks(a):
        return a.reshape(B, n, HG_CHUNK, H, a.shape[-1]).transpose(1, 0, 3, 2, 4)

    qc, kc, vc, gc = to_chunks(q), to_chunks(k), to_chunks(v), to_chunks(log_f)
    lower = jnp.tril(jnp.ones((HG_CHUNK, HG_CHUNK), dtype=bool))[:, :, None]
    lower_f = lower.astype(jnp.float32)

    def step(state, inp):
        qi, ki, vi, gi = inp
        b = jnp.cumsum(gi, axis=-2)
        o_inter = jnp.einsum('bhcd,bhde->bhce', qi * jnp.exp(b), state)
        diff = b[:, :, :, None, :] - b[:, :, None, :, :]
        decay = jnp.exp(jnp.where(lower, diff, 0.0)) * lower_f
        scores = jnp.einsum('bhid,bhjd,bhijd->bhij', qi, ki, decay)
        o_intra = jnp.einsum('bhij,bhje->bhie', scores, vi)
        b_last = b[:, :, -1:, :]
        new_state = (jnp.exp(b_last[:, :, 0, :])[..., None] * state
                     + jnp.einsum('bhcd,bhce->bhde', ki * jnp.exp(b_last - b), vi))
        return new_state, o_inter + o_intra

    init = jnp.zeros((B, H, Dk, Dv), jnp.float32)
    _, o = lax.scan(step, init, (qc, kc, vc, gc))
    return o.transpose(1, 0, 3, 2, 4).reshape(B, S, H, Dv)


def hgrn2_forget(z, lb):
    f = lb + (1.0 - lb) * jax.nn.sigmoid(z)
    log_f = jnp.log(jnp.maximum(f, MIN_FORGET))
    one_minus_f = (1.0 - lb) * jax.nn.sigmoid(-z)
    return log_f, one_minus_f


def token_mixer(h, lb, w_in, q_norm_w, k_norm_w, hg_norm_w, w_branch_att, w_branch_hg, w_out,
                rope_cos, rope_sin):
    B, S, D = h.shape
    f32 = jnp.float32
    split_idx = np.cumsum(IN_SPLITS)[:-1].tolist()
    proj = jnp.einsum('bsd,de->bse', h, w_in)
    aq, ak, av, hq, hz_f, hz_b, hi, hgate, gate_logits = jnp.split(proj, split_idx, axis=-1)

    q = aq.astype(f32).reshape(B, S, ATT_HEADS, ATT_HEAD_DIM)
    k = ak.astype(f32).reshape(B, S, ATT_KV_HEADS, ATT_HEAD_DIM)
    v = av.astype(f32).reshape(B, S, ATT_KV_HEADS, ATT_HEAD_DIM)
    q = apply_axial_rope(rms_norm(q, q_norm_w), rope_cos, rope_sin)
    k = apply_axial_rope(rms_norm(k, k_norm_w), rope_cos, rope_sin)
    o_att = blocked_bidirectional_gqa(q, k, v)

    lb = lb.astype(f32)
    hqh = jax.nn.silu(hq.astype(f32)).reshape(B, S, HG_HEADS, HG_F_DIM)
    hv = hi.astype(f32).reshape(B, S, HG_HEADS, HG_I_DIM)
    logf_f, kf = hgrn2_forget(hz_f.astype(f32), lb[0])
    logf_b, kb = hgrn2_forget(hz_b.astype(f32), lb[1])
    shp = (B, S, HG_HEADS, HG_F_DIM)
    o_fwd = gla_chunk_scan(hqh, kf.reshape(shp), hv, logf_f.reshape(shp))
    flip = lambda a: jnp.flip(a, axis=1)
    o_bwd = flip(gla_chunk_scan(flip(hqh), flip(kb.reshape(shp)), flip(hv), flip(logf_b.reshape(shp))))
    o_h = rms_norm(o_fwd + o_bwd, hg_norm_w) * jax.nn.silu(
        hgate.astype(f32).reshape(B, S, HG_HEADS, HG_I_DIM))
    o_hg = o_h.reshape(B, S, HG_WIDTH)

    ya = jnp.einsum('bsw,wd->bsd', o_att.astype(h.dtype), w_branch_att)
    yh = jnp.einsum('bsw,wd->bsd', o_hg.astype(h.dtype), w_branch_hg)
    g = jax.nn.sigmoid(gate_logits.astype(f32)).reshape(B, S, N_BRANCHES, D)
    merged = (g[:, :, 0] * ya.astype(f32) + g[:, :, 1] * yh.astype(f32)).astype(h.dtype)
    return jnp.einsum('bsd,de->bse', merged, w_out)


def expert_choice_ffn(h, w_router, w_gate, w_up, w_down):
    B, S, D = h.shape
    cap = EC_CAPACITY_FACTOR * S // N_EXPERTS
    logits = jnp.einsum('bsd,de->bse', h.astype(jnp.float32), w_router.astype(jnp.float32))
    affinity = jax.nn.softmax(logits, axis=-1)
    top_w, top_idx = lax.top_k(affinity.transpose(0, 2, 1), cap)
    xe = jax.vmap(lambda hb, ib: hb[ib])(h, top_idx)
    hid = jax.nn.silu(jnp.einsum('becd,edf->becf', xe, w_gate)) * jnp.einsum('becd,edf->becf', xe, w_up)
    ye = jnp.einsum('becf,efd->becd', hid, w_down) * top_w[..., None].astype(h.dtype)
    combine = lambda yb, ib: jnp.zeros((S, D), h.dtype).at[ib.reshape(-1)].add(yb.reshape(-1, D))
    return jax.vmap(combine)(ye, top_idx)


def setup_inputs(seed: int = 0) -> dict:
    key = jax.random.key(seed)
    ks = jax.random.split(key, 20)
    L, D = DEPTH, D_MODEL
    f32 = jnp.float32

    def nrm(k, shape, fan_in, scale=1.0):
        return jax.random.normal(k, shape, f32) * (scale * fan_in ** -0.5)

    def gain(k, shape):
        return 1.0 + 0.02 * jax.random.normal(k, shape, f32)

    return {
        "x": jax.random.normal(ks[0], (BATCH, SEQ, D), f32),
        "c": jax.random.normal(ks[1], (BATCH, D), f32),
        "ada_w": nrm(ks[2], (L, D, 6 * D), D, 0.5),
        "ada_b": 0.02 * jax.random.normal(ks[3], (L, 6 * D), f32),
        "norm_mix_w": gain(ks[4], (L, D)),
        "norm_ffn_w": gain(ks[5], (L, D)),
        "w_in": nrm(ks[6], (L, D, IN_WIDTH), D),
        "q_norm_w": gain(ks[7], (L, ATT_HEAD_DIM)),
        "k_norm_w": gain(ks[8], (L, ATT_HEAD_DIM)),
        "hg_lower_bounds": jax.random.normal(ks[9], (L, 2, HG_F_WIDTH), f32),
        "hg_norm_w": gain(ks[10], (L, HG_I_DIM)),
        "w_branch_att": nrm(ks[11], (L, ATT_WIDTH, D), ATT_WIDTH),
        "w_branch_hg": nrm(ks[12], (L, HG_WIDTH, D), HG_WIDTH),
        "w_out": nrm(ks[13], (L, D, D), D),
        "w_router": nrm(ks[14], (L, D, N_EXPERTS), D),
        "w_exp_gate": nrm(ks[15], (L, N_EXPERTS, D, EXPERT_FF), D),
        "w_exp_up": nrm(ks[16], (L, N_EXPERTS, D, EXPERT_FF), D),
        "w_exp_down": nrm(ks[17], (L, N_EXPERTS, EXPERT_FF, D), EXPERT_FF),
    }


def reference(x, c, ada_w, ada_b, norm_mix_w, norm_ffn_w, w_in, q_norm_w, k_norm_w,
              hg_lower_bounds, hg_norm_w, w_branch_att, w_branch_hg, w_out, w_router,
              w_exp_gate, w_exp_up, w_exp_down):
    rope_cos, rope_sin = axial_rope_tables(x.shape[1])
    lb_soft = jax.nn.softmax(hg_lower_bounds.astype(jnp.float32), axis=0)
    lb_all = jnp.cumsum(lb_soft, axis=0) - lb_soft[0]
    c_act = jax.nn.silu(c)
    for l in range(DEPTH):
        mod = (jnp.einsum('bd,de->be', c_act, ada_w[l]) + ada_b[l])[:, None, :]
        shift1, scale1, gate1, shift2, scale2, gate2 = jnp.split(mod, 6, axis=-1)
        h = rms_norm(x, norm_mix_w[l]) * (1.0 + scale1) + shift1
        x = x + gate1 * token_mixer(h, lb_all[l], w_in[l], q_norm_w[l], k_norm_w[l], hg_norm_w[l],
                                    w_branch_att[l], w_branch_hg[l], w_out[l], rope_cos, rope_sin)
        h = rms_norm(x, norm_ffn_w[l]) * (1.0 + scale2) + shift2
        x = x + gate2 * expert_choice_ffn(h, w_router[l], w_exp_gate[l], w_exp_up[l], w_exp_down[l])
    return x
```

```python
import functools

import numpy as np
import jax
import jax.numpy as jnp
from jax import lax
from jax.experimental import pallas as pl
from jax.experimental.pallas import tpu as pltpu

F32 = jnp.float32
BF16 = jnp.bfloat16
I32 = jnp.int32
HIGHEST = lax.Precision.HIGHEST

GRID_W = 64
ATT_HEADS = 8
ATT_KV_HEADS = 2
ATT_GROUP = ATT_HEADS // ATT_KV_HEADS
ATT_HEAD_DIM = 64
ATT_WIDTH = ATT_HEADS * ATT_HEAD_DIM
ATT_KV_WIDTH = ATT_KV_HEADS * ATT_HEAD_DIM
ROPE_AXIS_DIM = ATT_HEAD_DIM // 2
ROPE_THETA = 10000.0
HG_HEADS = 4
HG_DIM = 128
HG_WIDTH = HG_HEADS * HG_DIM
MIN_FORGET = 1e-6
N_EXPERTS = 16
EC_CAPACITY_FACTOR = 2
NORM_EPS = 1e-6

LANES = 128
VMEM_LIMIT_BYTES = 56 * 1024 * 1024

TOKEN_BLOCK = 512
ATT_Q_BLOCK = 256
ATT_KV_BLOCK = 512
HG_CHUNK = 128
HG_SUB = 16
ROUTE_BLOCK = 512
ROW_CHUNK = 256
FF_BLOCK = 512


def _cparams(*sem):
    return pltpu.CompilerParams(dimension_semantics=sem, vmem_limit_bytes=VMEM_LIMIT_BYTES)


def _sigmoid(x):
    return 1.0 / (1.0 + jnp.exp(-x))


def _silu(x):
    return x * _sigmoid(x)


def _full(shape):
    return pl.BlockSpec(shape, lambda *_: (0,) * len(shape))


def _ada_kernel(c_ref, w_ref, b_ref, o_ref):
    ca = _silu(c_ref[...])
    o_ref[0] = jnp.dot(ca, w_ref[0], precision=HIGHEST, preferred_element_type=F32) + b_ref[0]


def _ada_mod(c, ada_w, ada_b):
    n_layers, d, w6 = ada_w.shape
    col = 1536
    c8 = jnp.broadcast_to(c, (8, d))
    out = pl.pallas_call(
        _ada_kernel,
        grid=(n_layers, w6 // col),
        in_specs=[
            pl.BlockSpec((8, d), lambda l, j: (0, 0)),
            pl.BlockSpec((1, d, col), lambda l, j: (l, 0, j)),
            pl.BlockSpec((1, 1, col), lambda l, j: (l, 0, j)),
        ],
        out_specs=pl.BlockSpec((1, 8, col), lambda l, j: (l, 0, j)),
        out_shape=jax.ShapeDtypeStruct((n_layers, 8, w6), F32),
        compiler_params=_cparams("parallel", "parallel"),
    )(c8, ada_w, ada_b.reshape(n_layers, 1, w6))
    return out[:, 0, :]


def _rms_mod(x, nw, scale, shift):
    ms = jnp.mean(x * x, axis=-1, keepdims=True)
    return (x * lax.rsqrt(ms + NORM_EPS) * nw) * (1.0 + scale) + shift


def _qkv_kernel(x_ref, sh_ref, sc_ref, nw_ref, w_ref, qnw_ref, knw_ref, cos_ref, sin_ref,
                bd_ref, h_ref, q_ref, k_ref, v_ref):
    h = _rms_mod(x_ref[...], nw_ref[...], sc_ref[...], sh_ref[...])
    hb = h.astype(BF16)
    h_ref[...] = hb
    proj = jnp.dot(hb, w_ref[...], preferred_element_type=F32)
    aq = proj[:, :ATT_WIDTH]
    ak = proj[:, ATT_WIDTH:ATT_WIDTH + ATT_KV_WIDTH]
    av = proj[:, ATT_WIDTH + ATT_KV_WIDTH:]
    bd = bd_ref[...]
    qms = jnp.dot(aq * aq, bd, precision=HIGHEST, preferred_element_type=F32)
    kms = jnp.dot(ak * ak, bd[:ATT_KV_WIDTH, :ATT_KV_WIDTH], precision=HIGHEST,
                  preferred_element_type=F32)
    qn = aq * lax.rsqrt(qms + NORM_EPS) * qnw_ref[...]
    kn = ak * lax.rsqrt(kms + NORM_EPS) * knw_ref[...]
    cos = cos_ref[...]
    sin = sin_ref[...]
    lane = lax.broadcasted_iota(I32, cos.shape, 1)
    first_half = (lane % ROPE_AXIS_DIM) < (ROPE_AXIS_DIM // 2)

    def rope(xs):
        up = pltpu.roll(xs, LANES - ROPE_AXIS_DIM // 2, 1)
        dn = pltpu.roll(xs, ROPE_AXIS_DIM // 2, 1)
        return xs * cos + jnp.where(first_half, up, dn) * sin

    q_scale = ATT_HEAD_DIM ** -0.5
    for j in range(ATT_WIDTH // LANES):
        qr = rope(qn[:, j * LANES:(j + 1) * LANES]) * q_scale
        q_ref[2 * j] = qr[:, :ATT_HEAD_DIM].astype(BF16)
        q_ref[2 * j + 1] = qr[:, ATT_HEAD_DIM:].astype(BF16)
    kr = rope(kn)
    for j in range(ATT_KV_HEADS):
        k_ref[j] = kr[:, j * ATT_HEAD_DIM:(j + 1) * ATT_HEAD_DIM].astype(BF16)
        v_ref[j] = av[:, j * ATT_HEAD_DIM:(j + 1) * ATT_HEAD_DIM].astype(BF16)


def _qkv_call(x, shift, scale, nw, w_qkv, qnw, knw, cos, sin, bd):
    s, d = x.shape
    t = min(TOKEN_BLOCK, s)
    row = lambda i: (i, 0)
    head = lambda i: (0, i, 0)
    return pl.pallas_call(
        _qkv_kernel,
        grid=(s // t,),
        in_specs=[
            pl.BlockSpec((t, d), row), _full((1, d)), _full((1, d)), _full((1, d)),
            _full(w_qkv.shape), _full((1, ATT_WIDTH)), _full((1, ATT_KV_WIDTH)),
            pl.BlockSpec((t, LANES), row), pl.BlockSpec((t, LANES), row), _full(bd.shape),
        ],
        out_specs=[
            pl.BlockSpec((t, d), row),
            pl.BlockSpec((ATT_HEADS, t, ATT_HEAD_DIM), head),
            pl.BlockSpec((ATT_KV_HEADS, t, ATT_HEAD_DIM), head),
            pl.BlockSpec((ATT_KV_HEADS, t, ATT_HEAD_DIM), head),
        ],
        out_shape=[
            jax.ShapeDtypeStruct((s, d), BF16),
            jax.ShapeDtypeStruct((ATT_HEADS, s, ATT_HEAD_DIM), BF16),
            jax.ShapeDtypeStruct((ATT_KV_HEADS, s, ATT_HEAD_DIM), BF16),
            jax.ShapeDtypeStruct((ATT_KV_HEADS, s, ATT_HEAD_DIM), BF16),
        ],
        compiler_params=_cparams("parallel"),
    )(x, shift, scale, nw, w_qkv, qnw, knw, cos, sin, bd)


def _hg_proj_kernel(layer, h_ref, w_ref, lb_ref, q_ref, kf_ref, gf_ref, kb_ref, gb_ref,
                    v_ref, og_ref):
    hb = h_ref[...]

    def proj(j):
        return jnp.dot(hb, w_ref[:, j * HG_WIDTH:(j + 1) * HG_WIDTH], preferred_element_type=F32)

    lb_raw = lb_ref[...]
    n_layers = lb_raw.shape[0]
    mx = lb_raw[0]
    for i in range(1, n_layers):
        mx = jnp.maximum(mx, lb_raw[i])
    ex = [jnp.exp(lb_raw[i] - mx) for i in range(n_layers)]
    den = ex[0]
    for i in range(1, n_layers):
        den = den + ex[i]
    lb = jnp.zeros_like(mx)
    for i in range(1, layer + 1):
        lb = lb + ex[i] / den

    def forget(z, lbd):
        f = lbd + (1.0 - lbd) * _sigmoid(z)
        return jnp.log(jnp.maximum(f, MIN_FORGET)), (1.0 - lbd) * _sigmoid(-z)

    q_ref[...] = _silu(proj(0))
    gf, kf = forget(proj(1), lb[0:1])
    gf_ref[...] = gf
    kf_ref[...] = kf
    gb, kb = forget(proj(2), lb[1:2])
    gb_ref[...] = gb
    kb_ref[...] = kb
    v_ref[...] = proj(3)
    og_ref[...] = _silu(proj(4))


def _hg_proj_call(layer, h, w_hg, lb_raw):
    s, d = h.shape
    t = min(TOKEN_BLOCK, s)
    row = lambda i: (i, 0)
    out = jax.ShapeDtypeStruct((s, HG_WIDTH), F32)
    return pl.pallas_call(
        functools.partial(_hg_proj_kernel, layer),
        grid=(s // t,),
        in_specs=[pl.BlockSpec((t, d), row), _full(w_hg.shape), _full(lb_raw.shape)],
        out_specs=[pl.BlockSpec((t, HG_WIDTH), row)] * 7,
        out_shape=[out] * 7,
        compiler_params=_cparams("parallel"),
    )(h, w_hg, lb_raw)


def _gate_kernel(h_ref, w_ref, g_ref):
    g_ref[...] = _sigmoid(jnp.dot(h_ref[...], w_ref[...], preferred_element_type=F32)).astype(BF16)


def _gate_call(h, w_gate):
    s, d = h.shape
    t = min(TOKEN_BLOCK, s)
    n = w_gate.shape[1]
    return pl.pallas_call(
        _gate_kernel,
        grid=(s // t,),
        in_specs=[pl.BlockSpec((t, d), lambda i: (i, 0)), _full(w_gate.shape)],
        out_specs=pl.BlockSpec((t, n), lambda i: (i, 0)),
        out_shape=jax.ShapeDtypeStruct((s, n), BF16),
        compiler_params=_cparams("parallel"),
    )(h, w_gate)


def _attn_kernel(q_ref, k_ref, v_ref, o_ref):
    g, tq, hd = q_ref.shape
    s = k_ref.shape[1]
    tk = min(ATT_KV_BLOCK, s)
    m_rows = g * tq
    q = q_ref[...].reshape(m_rows, hd)

    def body(j, carry):
        m, l, acc = carry
        start = pl.multiple_of(j * tk, tk)
        ks = k_ref[0, pl.ds(start, tk), :]
        vs = v_ref[0, pl.ds(start, tk), :]
        sc = lax.dot_general(q, ks, (((1,), (1,)), ((), ())), preferred_element_type=F32)
        m_new = jnp.maximum(m, jnp.max(sc, axis=1, keepdims=True))
        alpha = jnp.exp(m - m_new)
        p = jnp.exp(sc - m_new)
        l = alpha * l + jnp.sum(p, axis=1, keepdims=True)
        acc = alpha * acc + jnp.dot(p.astype(BF16), vs, preferred_element_type=F32)
        return m_new, l, acc

    init = (jnp.full((m_rows, 1), -jnp.inf, F32), jnp.zeros((m_rows, 1), F32),
            jnp.zeros((m_rows, hd), F32))
    _, l, acc = lax.fori_loop(0, s // tk, body, init)
    o = acc / l
    o_ref[...] = jnp.concatenate([o[i * tq:(i + 1) * tq] for i in range(g)], axis=1).astype(BF16)


def _attn_call(q, k, v):
    _, s, hd = q.shape
    tq = min(ATT_Q_BLOCK, s)
    return pl.pallas_call(
        _attn_kernel,
        grid=(ATT_KV_HEADS, s // tq),
        in_specs=[
            pl.BlockSpec((ATT_GROUP, tq, hd), lambda h, i: (h, i, 0)),
            pl.BlockSpec((1, s, hd), lambda h, i: (h, 0, 0)),
            pl.BlockSpec((1, s, hd), lambda h, i: (h, 0, 0)),
        ],
        out_specs=pl.BlockSpec((tq, ATT_GROUP * hd), lambda h, i: (i, h)),
        out_shape=jax.ShapeDtypeStruct((s, ATT_HEADS * hd), BF16),
        compiler_params=_cparams("parallel", "parallel"),
    )(q, k, v)


def _hg_scan_body(reverse, q_ref, k_ref, v_ref, g_ref, tri_ref, st_ref):
    c = q_ref.shape[0]
    q = q_ref[...]
    k = k_ref[...]
    v = v_ref[...]
    vb = v.astype(BF16)
    b = jnp.dot(tri_ref[...], g_ref[...], precision=HIGHEST, preferred_element_type=F32)
    edge = c - 1 if not reverse else 0
    b_edge = b[edge:edge + 1]
    st = st_ref[...]
    o = lax.dot_general((q * jnp.exp(b)).astype(BF16), st.astype(BF16),
                        (((1,), (1,)), ((), ())), preferred_element_type=F32)
    kh = (k * jnp.exp(b_edge - b)).astype(BF16)
    st_ref[...] = st * jnp.exp(b_edge) + lax.dot_general(
        vb, kh, (((0,), (0,)), ((), ())), preferred_element_type=F32)

    lane = lax.broadcasted_iota(I32, (HG_SUB, c), 1)
    sub = lax.broadcasted_iota(I32, (HG_SUB, c), 0)
    rows = []
    for blk in range(c // HG_SUB):
        s0 = blk * HG_SUB
        bi = b[s0:s0 + HG_SUB]
        qi = q[s0:s0 + HG_SUB]
        ki = k[s0:s0 + HG_SUB]
        a_blk = jnp.zeros((HG_SUB, c), F32)
        has_off = blk > 0 if not reverse else blk < c // HG_SUB - 1
        if has_off:
            rr = s0 - 1 if not reverse else s0 + HG_SUB
            r = b[rr:rr + 1]
            qt = (qi * jnp.exp(bi - r)).astype(BF16)
            kt = (k * jnp.exp(jnp.minimum(r - b, 0.0))).astype(BF16)
            off = lax.dot_general(qt, kt, (((1,), (1,)), ((), ())), preferred_element_type=F32)
            keep = lane < s0 if not reverse else lane >= s0 + HG_SUB
            a_blk = jnp.where(keep, off, 0.0)
        for j in range(HG_SUB):
            dec = jnp.exp(jnp.minimum(bi - bi[j:j + 1], 0.0))
            col = jnp.sum(dec * qi * ki[j:j + 1], axis=1, keepdims=True)
            ok = (sub >= j) if not reverse else (sub <= j)
            a_blk = a_blk + jnp.where((lane == s0 + j) & ok, col, 0.0)
        rows.append(a_blk)
    a = jnp.concatenate(rows, axis=0).astype(BF16)
    return o + jnp.dot(a, vb, preferred_element_type=F32)


def _hg_fwd_kernel(q_ref, k_ref, v_ref, g_ref, tri_ref, o_ref, st_ref):
    @pl.when(pl.program_id(1) == 0)
    def _():
        st_ref[...] = jnp.zeros_like(st_ref)

    o_ref[...] = _hg_scan_body(False, q_ref, k_ref, v_ref, g_ref, tri_ref, st_ref)


def _hg_bwd_kernel(q_ref, k_ref, v_ref, g_ref, tri_ref, of_ref, og_ref, nw_ref, o_ref, st_ref):
    @pl.when(pl.program_id(1) == 0)
    def _():
        st_ref[...] = jnp.zeros_like(st_ref)

    o = of_ref[...] + _hg_scan_body(True, q_ref, k_ref, v_ref, g_ref, tri_ref, st_ref)
    ms = jnp.mean(o * o, axis=-1, keepdims=True)
    o_ref[...] = ((o * lax.rsqrt(ms + NORM_EPS) * nw_ref[...]) * og_ref[...]).astype(BF16)


def _hg_call(q, kf, gf, kb, gb, v, og, nw):
    s = q.shape[0]
    c = min(HG_CHUNK, s)
    n = s // c
    ii = np.arange(c)
    tri_f = jnp.asarray((ii[:, None] >= ii[None, :]).astype(np.float32))
    tri_b = jnp.asarray((ii[:, None] <= ii[None, :]).astype(np.float32))
    fwd = lambda h, i: (i, h)
    bwd = lambda h, i: (n - 1 - i, h)
    blk = lambda m: pl.BlockSpec((c, HG_DIM), m)
    scratch = [pltpu.VMEM((HG_DIM, HG_DIM), F32)]
    o_fwd = pl.pallas_call(
        _hg_fwd_kernel,
        grid=(HG_HEADS, n),
        in_specs=[blk(fwd)] * 4 + [_full((c, c))],
        out_specs=blk(fwd),
        out_shape=jax.ShapeDtypeStruct((s, HG_WIDTH), F32),
        scratch_shapes=scratch,
        compiler_params=_cparams("parallel", "arbitrary"),
    )(q, kf, v, gf, tri_f)
    return pl.pallas_call(
        _hg_bwd_kernel,
        grid=(HG_HEADS, n),
        in_specs=[blk(bwd)] * 4 + [_full((c, c)), blk(bwd), blk(bwd), _full((1, HG_DIM))],
        out_specs=blk(bwd),
        out_shape=jax.ShapeDtypeStruct((s, HG_WIDTH), BF16),
        scratch_shapes=scratch,
        compiler_params=_cparams("parallel", "arbitrary"),
    )(q, kb, v, gb, tri_b, o_fwd, og, nw)


def _merge_kernel(x_ref, oa_ref, oh_ref, g_ref, wa_ref, wh_ref, wo_ref, g1_ref, sh_ref, sc_ref,
                  nw_ref, wr_ref, xo_ref, h_ref, aff_ref):
    d = x_ref.shape[1]
    ya = jnp.dot(oa_ref[...], wa_ref[...], preferred_element_type=F32)
    yh = jnp.dot(oh_ref[...], wh_ref[...], preferred_element_type=F32)
    merged = g_ref[:, :d].astype(F32) * ya + g_ref[:, d:].astype(F32) * yh
    y = jnp.dot(merged.astype(BF16), wo_ref[...], preferred_element_type=F32)
    xn = x_ref[...] + g1_ref[...] * y
    xo_ref[...] = xn
    h = _rms_mod(xn, nw_ref[...], sc_ref[...], sh_ref[...])
    h_ref[...] = h.astype(BF16)
    logits = lax.dot_general(wr_ref[...], h, (((1,), (1,)), ((), ())), precision=HIGHEST,
                             preferred_element_type=F32)
    ex = jnp.exp(logits - jnp.max(logits, axis=0, keepdims=True))
    aff_ref[...] = ex / jnp.sum(ex, axis=0, keepdims=True)


def _merge_call(x, o_att, o_hg, g, w_ba, w_bh, w_out, gate1, shift2, scale2, nw2, w_router_t):
    s, d = x.shape
    t = min(TOKEN_BLOCK, s)
    row = lambda i: (i, 0)
    return pl.pallas_call(
        _merge_kernel,
        grid=(s // t,),
        in_specs=[
            pl.BlockSpec((t, d), row), pl.BlockSpec((t, ATT_WIDTH), row),
            pl.BlockSpec((t, HG_WIDTH), row), pl.BlockSpec((t, 2 * d), row),
            _full(w_ba.shape), _full(w_bh.shape), _full(w_out.shape),
            _full((1, d)), _full((1, d)), _full((1, d)), _full((1, d)), _full(w_router_t.shape),
        ],
        out_specs=[pl.BlockSpec((t, d), row), pl.BlockSpec((t, d), row),
                   pl.BlockSpec((N_EXPERTS, t), lambda i: (0, i))],
        out_shape=[jax.ShapeDtypeStruct((s, d), F32), jax.ShapeDtypeStruct((s, d), BF16),
                   jax.ShapeDtypeStruct((N_EXPERTS, s), F32)],
        compiler_params=_cparams("parallel"),
    )(x, o_att, o_hg, g, w_ba, w_bh, w_out, gate1, shift2, scale2, nw2, w_router_t)


def _select_kernel(cap, aff_ref, tri_ref, pos_ref, starts_ref):
    n_exp, s = aff_ref.shape
    t = tri_ref.shape[0]
    n_blk = s // t
    aff = aff_ref[...]

    def bit_step(i, thr_bits):
        cand = thr_bits | lax.shift_left(jnp.int32(1), jnp.int32(30) - i)
        cnt = jnp.sum((aff >= pltpu.bitcast(cand, F32)).astype(F32), axis=1, keepdims=True)
        return jnp.where(cnt >= cap, cand, thr_bits)

    thr = pltpu.bitcast(lax.fori_loop(0, 31, bit_step, jnp.zeros((n_exp, 1), I32)), F32)
    n_gt = jnp.sum((aff > thr).astype(F32), axis=1, keepdims=True)
    need_eq = cap - n_gt
    tri = tri_ref[...]
    blk_lane = lax.broadcasted_iota(I32, starts_ref.shape, 1)

    starts_ref[...] = jnp.zeros(starts_ref.shape, I32)

    def blk_step(j, carry):
        c_eq, c_sel = carry
        start = pl.multiple_of(j * t, t)
        bt = aff_ref[:, pl.ds(start, t)]
        eq = bt == thr
        eq_f = eq.astype(BF16)
        eq_excl = c_eq + jnp.dot(eq_f, tri, preferred_element_type=F32) - eq_f.astype(F32)
        sel = (bt > thr) | (eq & (eq_excl < need_eq))
        sel_f = sel.astype(BF16)
        sel_incl = c_sel + jnp.dot(sel_f, tri, preferred_element_type=F32)
        pos_ref[:, pl.ds(start, t)] = jnp.where(sel, sel_incl - 1.0, -1.0).astype(I32)
        starts_ref[...] = jnp.where(blk_lane == j, c_sel.astype(I32), starts_ref[...])
        return (c_eq + jnp.sum(eq_f.astype(F32), axis=1, keepdims=True),
                c_sel + jnp.sum(sel_f.astype(F32), axis=1, keepdims=True))

    zero = jnp.zeros((n_exp, 1), F32)
    _, c_sel = lax.fori_loop(0, n_blk, blk_step, (zero, zero))
    starts_ref[...] = jnp.where(blk_lane >= n_blk, c_sel.astype(I32), starts_ref[...])


def _select_call(aff_t, cap):
    n_exp, s = aff_t.shape
    t = min(ROUTE_BLOCK, s)
    ii = np.arange(t)
    tri = jnp.asarray((ii[:, None] <= ii[None, :]).astype(np.float32), dtype=BF16)
    return pl.pallas_call(
        functools.partial(_select_kernel, float(cap)),
        grid=(1,),
        in_specs=[_full((n_exp, s)), _full((t, t))],
        out_specs=[_full((n_exp, s)), _full((n_exp, LANES))],
        out_shape=[jax.ShapeDtypeStruct((n_exp, s), I32), jax.ShapeDtypeStruct((n_exp, LANES), I32)],
        compiler_params=_cparams("arbitrary"),
    )(aff_t, tri)


def _gather_kernel(cap, starts_ref, pos_ref, h_ref, xe_ref, acc_ref):
    e = pl.program_id(0)
    b = pl.program_id(1)
    t = h_ref.shape[0]

    @pl.when(b == 0)
    def _():
        acc_ref[...] = jnp.zeros_like(acc_ref)

    lo = starts_ref[e, b]
    hi = starts_ref[e, b + 1]
    c0 = lo // ROW_CHUNK
    pos = pos_ref[pl.ds(e, 1), :]
    slot = lax.broadcasted_iota(I32, (ROW_CHUNK, t), 0)
    for kk in range(t // ROW_CHUNK + 1):
        c = c0 + kk

        @pl.when((hi > lo) & (c * ROW_CHUNK < hi) & (c < cap // ROW_CHUNK))
        def _():
            base = pl.multiple_of(c * ROW_CHUNK, ROW_CHUNK)
            onehot = (pos - base == slot).astype(BF16)
            acc_ref[pl.ds(base, ROW_CHUNK), :] += jnp.dot(onehot, h_ref[...],
                                                          preferred_element_type=F32)

    @pl.when(b == pl.num_programs(1) - 1)
    def _():
        xe_ref[0] = acc_ref[...].astype(BF16)


def _gather_call(starts, pos_t, h, cap):
    s, d = h.shape
    t = min(ROUTE_BLOCK, s)
    return pl.pallas_call(
        functools.partial(_gather_kernel, cap),
        grid_spec=pltpu.PrefetchScalarGridSpec(
            num_scalar_prefetch=1,
            grid=(N_EXPERTS, s // t),
            in_specs=[pl.BlockSpec((N_EXPERTS, t), lambda e, b, st: (0, b)),
                      pl.BlockSpec((t, d), lambda e, b, st: (b, 0))],
            out_specs=pl.BlockSpec((1, cap, d), lambda e, b, st: (e, 0, 0)),
            scratch_shapes=[pltpu.VMEM((cap, d), F32)],
        ),
        out_shape=jax.ShapeDtypeStruct((N_EXPERTS, cap, d), BF16),
        compiler_params=_cparams("parallel", "arbitrary"),
    )(starts, pos_t, h)


def _ffn_kernel(xe_ref, wg_ref, wu_ref, wd_ref, ye_ref, acc_ref):
    f = pl.program_id(1)
    x = xe_ref[0]
    a = jnp.dot(x, wg_ref[0].astype(BF16), preferred_element_type=F32)
    u = jnp.dot(x, wu_ref[0].astype(BF16), preferred_element_type=F32)
    hid = (_silu(a) * u).astype(BF16)
    contrib = jnp.dot(hid, wd_ref[0].astype(BF16), preferred_element_type=F32)

    @pl.when(f == 0)
    def _():
        acc_ref[...] = contrib

    @pl.when(f > 0)
    def _():
        acc_ref[...] += contrib

    @pl.when(f == pl.num_programs(1) - 1)
    def _():
        ye_ref[0] = acc_ref[...].astype(BF16)


def _ffn_call(xe, w_gate, w_up, w_down):
    n_exp, cap, d = xe.shape
    ff = w_gate.shape[2]
    fb = min(FF_BLOCK, ff)
    return pl.pallas_call(
        _ffn_kernel,
        grid=(n_exp, ff // fb),
        in_specs=[pl.BlockSpec((1, cap, d), lambda e, f: (e, 0, 0)),
                  pl.BlockSpec((1, d, fb), lambda e, f: (e, 0, f)),
                  pl.BlockSpec((1, d, fb), lambda e, f: (e, 0, f)),
                  pl.BlockSpec((1, fb, d), lambda e, f: (e, f, 0))],
        out_specs=pl.BlockSpec((1, cap, d), lambda e, f: (e, 0, 0)),
        out_shape=jax.ShapeDtypeStruct((n_exp, cap, d), BF16),
        scratch_shapes=[pltpu.VMEM((cap, d), F32)],
        compiler_params=_cparams("parallel", "arbitrary"),
    )(xe, w_gate, w_up, w_down)


def _combine_kernel(cap, n_cand, starts_ref, pos_ref, w_ref, x_ref, g2_ref, *rest):
    ye_refs = rest[:n_cand]
    xo_ref, acc_ref = rest[n_cand:]
    b = pl.program_id(0)
    e = pl.program_id(1)
    t = x_ref.shape[0]

    @pl.when(e == 0)
    def _():
        acc_ref[...] = jnp.zeros_like(acc_ref)

    lo = starts_ref[e, b]
    hi = starts_ref[e, b + 1]
    c0 = lo // ROW_CHUNK
    exp_lane = lax.broadcasted_iota(I32, pos_ref.shape, 1) == e
    pos = jnp.sum(jnp.where(exp_lane, pos_ref[...], 0), axis=1, keepdims=True)
    wgt = jnp.sum(jnp.where(exp_lane, w_ref[...], 0.0), axis=1, keepdims=True)
    slot = lax.broadcasted_iota(I32, (t, ROW_CHUNK), 1)
    for kk in range(n_cand):
        c = c0 + kk

        @pl.when((hi > lo) & (c * ROW_CHUNK < hi) & (c < cap // ROW_CHUNK))
        def _():
            onehot = (pos - c * ROW_CHUNK == slot).astype(BF16)
            acc_ref[...] += wgt * jnp.dot(onehot, ye_refs[kk][0], preferred_element_type=F32)

    @pl.when(e == pl.num_programs(1) - 1)
    def _():
        xo_ref[...] = x_ref[...] + g2_ref[...] * acc_ref[...]


def _combine_call(starts, pos_tok, w_tok, x, gate2, ye):
    s, d = x.shape
    n_exp, cap, _ = ye.shape
    t = min(ROUTE_BLOCK, s)
    n_chunks = cap // ROW_CHUNK
    n_cand = min(t // ROW_CHUNK + 1, n_chunks)

    def ye_spec(kk):
        return pl.BlockSpec(
            (1, ROW_CHUNK, d),
            lambda b, e, st: (e, jnp.minimum(st[e, b] // ROW_CHUNK + kk, n_chunks - 1), 0))

    return pl.pallas_call(
        functools.partial(_combine_kernel, cap, n_cand),
        grid_spec=pltpu.PrefetchScalarGridSpec(
            num_scalar_prefetch=1,
            grid=(s // t, n_exp),
            in_specs=[pl.BlockSpec((t, n_exp), lambda b, e, st: (b, 0)),
                      pl.BlockSpec((t, n_exp), lambda b, e, st: (b, 0)),
                      pl.BlockSpec((t, d), lambda b, e, st: (b, 0)),
                      pl.BlockSpec((1, d), lambda b, e, st: (0, 0))]
            + [ye_spec(kk) for kk in range(n_cand)],
            out_specs=pl.BlockSpec((t, d), lambda b, e, st: (b, 0)),
            scratch_shapes=[pltpu.VMEM((t, d), F32)],
        ),
        out_shape=jax.ShapeDtypeStruct((s, d), F32),
        compiler_params=_cparams("parallel", "arbitrary"),
    )(starts, pos_tok, w_tok, x, gate2, *([ye] * n_cand))


def _rope_tables(seq_len):
    rows = seq_len // GRID_W
    row_id = jnp.repeat(jnp.arange(rows, dtype=F32), GRID_W)
    col_id = jnp.tile(jnp.arange(GRID_W, dtype=F32), rows)
    inv_freq = ROPE_THETA ** (-jnp.arange(0, ROPE_AXIS_DIM, 2, dtype=F32) / ROPE_AXIS_DIM)
    ang = jnp.stack([row_id[:, None] * inv_freq, col_id[:, None] * inv_freq], axis=1)
    cos, sin = jnp.cos(ang), jnp.sin(ang)
    cos_h = jnp.concatenate([cos, cos], axis=-1).reshape(seq_len, ATT_HEAD_DIM)
    sin_h = jnp.concatenate([-sin, sin], axis=-1).reshape(seq_len, ATT_HEAD_DIM)
    reps = LANES // ATT_HEAD_DIM
    return jnp.tile(cos_h, (1, reps)), jnp.tile(sin_h, (1, reps))


def _head_mean_operator():
    ii = np.arange(ATT_WIDTH)
    same = (ii[:, None] // ATT_HEAD_DIM) == (ii[None, :] // ATT_HEAD_DIM)
    return jnp.asarray(same.astype(np.float32) / ATT_HEAD_DIM)


def kernel(x, c, ada_w, ada_b, norm_mix_w, norm_ffn_w, w_in, q_norm_w, k_norm_w, hg_lower_bounds,
           hg_norm_w, w_branch_att, w_branch_hg, w_out, w_router, w_exp_gate, w_exp_up, w_exp_down):
    batch, s, d = x.shape
    assert batch == 1, "kernel is written for a single sequence"
    n_layers = ada_w.shape[0]
    cap = EC_CAPACITY_FACTOR * s // N_EXPERTS
    assert cap % ROW_CHUNK == 0 and s % min(ROUTE_BLOCK, s) == 0

    cos, sin = _rope_tables(s)
    bd = _head_mean_operator()
    mods = _ada_mod(c, ada_w, ada_b)
    qkv_w = ATT_WIDTH + 2 * ATT_KV_WIDTH
    hg_w = 5 * HG_WIDTH
    xs = x[0]
    for l in range(n_layers):
        shift1, scale1, gate1, shift2, scale2, gate2 = [
            mods[l, i * d:(i + 1) * d].reshape(1, d) for i in range(6)]
        w_in_b = w_in[l].astype(BF16)
        h, q, k, v = _qkv_call(
            xs, shift1, scale1, norm_mix_w[l].reshape(1, d), w_in_b[:, :qkv_w],
            jnp.tile(q_norm_w[l], ATT_HEADS).reshape(1, ATT_WIDTH),
            jnp.tile(k_norm_w[l], ATT_KV_HEADS).reshape(1, ATT_KV_WIDTH), cos, sin, bd)
        hq, kf, gf, kb, gb, hv, og = _hg_proj_call(l, h, w_in_b[:, qkv_w:qkv_w + hg_w],
                                                   hg_lower_bounds)
        g = _gate_call(h, w_in_b[:, qkv_w + hg_w:])
        o_att = _attn_call(q, k, v)
        o_hg = _hg_call(hq, kf, gf, kb, gb, hv, og, hg_norm_w[l].reshape(1, HG_DIM))
        xs, h2, aff_t = _merge_call(
            xs, o_att, o_hg, g, w_branch_att[l].astype(BF16), w_branch_hg[l].astype(BF16),
            w_out[l].astype(BF16), gate1, shift2, scale2, norm_ffn_w[l].reshape(1, d),
            w_router[l].T)
        pos_t, starts = _select_call(aff_t, cap)
        xe = _gather_call(starts, pos_t, h2, cap)
        ye = _ffn_call(xe, w_exp_gate[l], w_exp_up[l], w_exp_down[l])
        xs = _combine_call(starts, pos_t.T, aff_t.T, xs, gate2, ye)
    return xs[None]
```

```python
import functools

import numpy as np
import jax
import jax.numpy as jnp
from jax import lax
from jax.experimental import pallas as pl
from jax.experimental.pallas import tpu as pltpu

F32 = jnp.float32
BF16 = jnp.bfloat16
I32 = jnp.int32
HIGHEST = lax.Precision.HIGHEST

GRID_W = 64
ATT_HEADS = 8
ATT_KV_HEADS = 2
ATT_GROUP = ATT_HEADS // ATT_KV_HEADS
ATT_HEAD_DIM = 64
ATT_WIDTH = ATT_HEADS * ATT_HEAD_DIM
ATT_KV_WIDTH = ATT_KV_HEADS * ATT_HEAD_DIM
ROPE_AXIS_DIM = ATT_HEAD_DIM // 2
ROPE_THETA = 10000.0
HG_HEADS = 4
HG_DIM = 128
HG_WIDTH = HG_HEADS * HG_DIM
MIN_FORGET = 1e-6
N_EXPERTS = 16
EC_CAPACITY_FACTOR = 2
NORM_EPS = 1e-6
LOG2_E = 1.4426950408889634

LANES = 128
VMEM_LIMIT_BYTES = 56 * 1024 * 1024

TOKEN_BLOCK = 512
ATT_Q_BLOCK = 512
ATT_KV_BLOCK = 512
ATT_ROW_PARTS = 4
ATT_ONES_ROWS = 16
HG_CHUNK = 128
HG_SUB = 16
ROUTE_BLOCK = 512
GATHER_BLOCK = 1024
ROW_CHUNK = 256
FF_BLOCK = 512


def _cparams(*sem):
    return pltpu.CompilerParams(dimension_semantics=sem, vmem_limit_bytes=VMEM_LIMIT_BYTES)


def _sigmoid(x):
    return 1.0 / (1.0 + jnp.exp(-x))


def _silu(x):
    return x * _sigmoid(x)


def _full(shape):
    return pl.BlockSpec(shape, lambda *_: (0,) * len(shape))


def _ada_kernel(c_ref, w_ref, b_ref, o_ref):
    ca = _silu(c_ref[...])
    o_ref[0] = jnp.dot(ca, w_ref[0], precision=HIGHEST, preferred_element_type=F32) + b_ref[0]


def _ada_mod(c, ada_w, ada_b):
    n_layers, d, w6 = ada_w.shape
    col = 1536
    c8 = jnp.broadcast_to(c, (8, d))
    out = pl.pallas_call(
        _ada_kernel,
        grid=(n_layers, w6 // col),
        in_specs=[
            pl.BlockSpec((8, d), lambda l, j: (0, 0)),
            pl.BlockSpec((1, d, col), lambda l, j: (l, 0, j)),
            pl.BlockSpec((1, 1, col), lambda l, j: (l, 0, j)),
        ],
        out_specs=pl.BlockSpec((1, 8, col), lambda l, j: (l, 0, j)),
        out_shape=jax.ShapeDtypeStruct((n_layers, 8, w6), F32),
        compiler_params=_cparams("parallel", "parallel"),
        name="ada_mod",
    )(c8, ada_w, ada_b.reshape(n_layers, 1, w6))
    return out[:, 0, :]


def _rms_mod(x, nw, scale, shift):
    ms = jnp.mean(x * x, axis=-1, keepdims=True)
    return (x * lax.rsqrt(ms + NORM_EPS) * nw) * (1.0 + scale) + shift


def _qkv_kernel(x_ref, sh_ref, sc_ref, nw_ref, w_ref, wvt_ref, qnw_ref, knw_ref, cos_ref,
                sin_ref, bd_ref, h_ref, q_ref, k_ref, vt_ref):
    h = _rms_mod(x_ref[...], nw_ref[...], sc_ref[...], sh_ref[...])
    hb = h.astype(BF16)
    h_ref[...] = hb
    proj = jnp.dot(hb, w_ref[...], preferred_element_type=F32)
    aq = proj[:, :ATT_WIDTH]
    ak = proj[:, ATT_WIDTH:]
    avt = lax.dot_general(wvt_ref[...], hb, (((1,), (1,)), ((), ())),
                          preferred_element_type=F32)
    bd = bd_ref[...]
    qms = jnp.dot(aq * aq, bd, precision=HIGHEST, preferred_element_type=F32)
    kms = jnp.dot(ak * ak, bd[:ATT_KV_WIDTH, :ATT_KV_WIDTH], precision=HIGHEST,
                  preferred_element_type=F32)
    qn = aq * lax.rsqrt(qms + NORM_EPS) * qnw_ref[...]
    kn = ak * lax.rsqrt(kms + NORM_EPS) * knw_ref[...]
    cos = cos_ref[...]
    sin = sin_ref[...]
    lane = lax.broadcasted_iota(I32, cos.shape, 1)
    first_half = (lane % ROPE_AXIS_DIM) < (ROPE_AXIS_DIM // 2)

    def rope(xs):
        up = pltpu.roll(xs, LANES - ROPE_AXIS_DIM // 2, 1)
        dn = pltpu.roll(xs, ROPE_AXIS_DIM // 2, 1)
        return xs * cos + jnp.where(first_half, up, dn) * sin

    q_scale = ATT_HEAD_DIM ** -0.5 * LOG2_E
    for j in range(ATT_WIDTH // LANES):
        qr = rope(qn[:, j * LANES:(j + 1) * LANES]) * q_scale
        q_ref[2 * j] = qr[:, :ATT_HEAD_DIM].astype(BF16)
        q_ref[2 * j + 1] = qr[:, ATT_HEAD_DIM:].astype(BF16)
    kr = rope(kn)
    ones = jnp.ones((ATT_ONES_ROWS, avt.shape[1]), F32)
    for j in range(ATT_KV_HEADS):
        k_ref[j] = kr[:, j * ATT_HEAD_DIM:(j + 1) * ATT_HEAD_DIM].astype(BF16)
        vj = avt[j * ATT_HEAD_DIM:(j + 1) * ATT_HEAD_DIM]
        vt_ref[j] = jnp.concatenate([vj, ones], axis=0).astype(BF16)


def _qkv_call(x, shift, scale, nw, w_qk, w_v_t, qnw, knw, cos, sin, bd):
    s, d = x.shape
    t = min(TOKEN_BLOCK, s)
    vr = ATT_HEAD_DIM + ATT_ONES_ROWS
    row = lambda i: (i, 0)
    head = lambda i: (0, i, 0)
    return pl.pallas_call(
        _qkv_kernel,
        grid=(s // t,),
        in_specs=[
            pl.BlockSpec((t, d), row), _full((1, d)), _full((1, d)), _full((1, d)),
            _full(w_qk.shape), _full(w_v_t.shape), _full((1, ATT_WIDTH)), _full((1, ATT_KV_WIDTH)),
            pl.BlockSpec((t, LANES), row), pl.BlockSpec((t, LANES), row), _full(bd.shape),
        ],
        out_specs=[
            pl.BlockSpec((t, d), row),
            pl.BlockSpec((ATT_HEADS, t, ATT_HEAD_DIM), head),
            pl.BlockSpec((ATT_KV_HEADS, t, ATT_HEAD_DIM), head),
            pl.BlockSpec((ATT_KV_HEADS, vr, t), lambda i: (0, 0, i)),
        ],
        out_shape=[
            jax.ShapeDtypeStruct((s, d), BF16),
            jax.ShapeDtypeStruct((ATT_HEADS, s, ATT_HEAD_DIM), BF16),
            jax.ShapeDtypeStruct((ATT_KV_HEADS, s, ATT_HEAD_DIM), BF16),
            jax.ShapeDtypeStruct((ATT_KV_HEADS, vr, s), BF16),
        ],
        compiler_params=_cparams("parallel"),
        name="qkv_proj",
    )(x, shift, scale, nw, w_qk, w_v_t, qnw, knw, cos, sin, bd)


def _hg_proj_kernel(layer, h_ref, w_ref, lb_ref, q_ref, kf_ref, gf_ref, kb_ref, gb_ref,
                    v_ref, og_ref):
    hb = h_ref[...]

    def proj(j):
        return jnp.dot(hb, w_ref[:, j * HG_WIDTH:(j + 1) * HG_WIDTH], preferred_element_type=F32)

    lb_raw = lb_ref[...]
    n_layers = lb_raw.shape[0]
    mx = lb_raw[0]
    for i in range(1, n_layers):
        mx = jnp.maximum(mx, lb_raw[i])
    ex = [jnp.exp(lb_raw[i] - mx) for i in range(n_layers)]
    den = ex[0]
    for i in range(1, n_layers):
        den = den + ex[i]
    lb = jnp.zeros_like(mx)
    for i in range(1, layer + 1):
        lb = lb + ex[i] / den

    def forget(z, lbd):
        f = lbd + (1.0 - lbd) * _sigmoid(z)
        return jnp.log(jnp.maximum(f, MIN_FORGET)), (1.0 - lbd) * _sigmoid(-z)

    q_ref[...] = _silu(proj(0))
    gf, kf = forget(proj(1), lb[0:1])
    gf_ref[...] = gf
    kf_ref[...] = kf
    gb, kb = forget(proj(2), lb[1:2])
    gb_ref[...] = gb
    kb_ref[...] = kb
    v_ref[...] = proj(3)
    og_ref[...] = _silu(proj(4))


def _hg_proj_call(layer, h, w_hg, lb_raw):
    s, d = h.shape
    t = min(TOKEN_BLOCK, s)
    row = lambda i: (i, 0)
    out = jax.ShapeDtypeStruct((s, HG_WIDTH), F32)
    return pl.pallas_call(
        functools.partial(_hg_proj_kernel, layer),
        grid=(s // t,),
        in_specs=[pl.BlockSpec((t, d), row), _full(w_hg.shape), _full(lb_raw.shape)],
        out_specs=[pl.BlockSpec((t, HG_WIDTH), row)] * 7,
        out_shape=[out] * 7,
        compiler_params=_cparams("parallel"),
        name="hg_proj",
    )(h, w_hg, lb_raw)


def _gate_kernel(h_ref, w_ref, g_ref):
    g_ref[...] = _sigmoid(jnp.dot(h_ref[...], w_ref[...], preferred_element_type=F32)).astype(BF16)


def _gate_call(h, w_gate):
    s, d = h.shape
    t = min(TOKEN_BLOCK, s)
    n = w_gate.shape[1]
    return pl.pallas_call(
        _gate_kernel,
        grid=(s // t,),
        in_specs=[pl.BlockSpec((t, d), lambda i: (i, 0)), _full(w_gate.shape)],
        out_specs=pl.BlockSpec((t, n), lambda i: (i, 0)),
        out_shape=jax.ShapeDtypeStruct((s, n), BF16),
        compiler_params=_cparams("parallel"),
        name="gate_proj",
    )(h, w_gate)


def _attn_kernel(q_ref, k_ref, vt_ref, o_ref, m_ref, acc_ref):
    g, tq, hd = q_ref.shape
    s = k_ref.shape[1]
    tk = min(ATT_KV_BLOCK, s)
    heads_per_part = g // ATT_ROW_PARTS
    cols = heads_per_part * tq
    m_ref[...] = jnp.full(m_ref.shape, -jnp.inf, F32)
    acc_ref[...] = jnp.zeros(acc_ref.shape, F32)

    def body(j, carry):
        start = pl.multiple_of(j * tk, tk)
        ks = k_ref[0, pl.ds(start, tk), :]
        vt = vt_ref[0, :, pl.ds(start, tk)]
        sts = []
        for r in range(ATT_ROW_PARTS):
            q = q_ref[r * heads_per_part:(r + 1) * heads_per_part].reshape(cols, hd)
            sts.append(lax.dot_general(ks, q, (((1,), (1,)), ((), ())),
                                       preferred_element_type=F32))
        for r in range(ATT_ROW_PARTS):
            sl = pl.ds(r * cols, cols)
            st = sts[r]
            m_old = m_ref[:, sl]
            m_new = jnp.maximum(m_old, jnp.max(st, axis=0, keepdims=True))
            p = jnp.exp2(st - m_new).astype(BF16)
            acc_ref[:, sl] = (jnp.exp2(m_old - m_new) * acc_ref[:, sl]
                              + jnp.dot(vt, p, preferred_element_type=F32))
            m_ref[:, sl] = m_new
        return carry

    lax.fori_loop(0, s // tk, body, 0)
    acc = acc_ref[...]
    o_t = acc[:hd] / acc[hd:hd + 1]
    for i in range(g):
        o_ref[:, i * hd:(i + 1) * hd] = o_t[:, i * tq:(i + 1) * tq].T.astype(BF16)


def _attn_call(q, k, vt):
    _, s, hd = q.shape
    vr = vt.shape[1]
    tq = min(ATT_Q_BLOCK, s)
    return pl.pallas_call(
        _attn_kernel,
        grid=(ATT_KV_HEADS, s // tq),
        in_specs=[
            pl.BlockSpec((ATT_GROUP, tq, hd), lambda h, i: (h, i, 0)),
            pl.BlockSpec((1, s, hd), lambda h, i: (h, 0, 0)),
            pl.BlockSpec((1, vr, s), lambda h, i: (h, 0, 0)),
        ],
        out_specs=pl.BlockSpec((tq, ATT_GROUP * hd), lambda h, i: (i, h)),
        out_shape=jax.ShapeDtypeStruct((s, ATT_HEADS * hd), BF16),
        scratch_shapes=[pltpu.VMEM((1, ATT_GROUP * tq), F32),
                        pltpu.VMEM((vr, ATT_GROUP * tq), F32)],
        compiler_params=_cparams("parallel", "parallel"),
        name="attention",
    )(q, k, vt)


def _hg_scan_body(reverse, q_ref, k_ref, v_ref, g_ref, tri_ref, st_ref):
    c = q_ref.shape[0]
    q = q_ref[...]
    k = k_ref[...]
    v = v_ref[...]
    vb = v.astype(BF16)
    b = jnp.dot(tri_ref[...], g_ref[...], precision=HIGHEST, preferred_element_type=F32)
    edge = c - 1 if not reverse else 0
    b_edge = b[edge:edge + 1]
    st = st_ref[...]
    o = lax.dot_general((q * jnp.exp(b)).astype(BF16), st.astype(BF16),
                        (((1,), (1,)), ((), ())), preferred_element_type=F32)
    kh = (k * jnp.exp(b_edge - b)).astype(BF16)
    st_ref[...] = st * jnp.exp(b_edge) + lax.dot_general(
        vb, kh, (((0,), (0,)), ((), ())), preferred_element_type=F32)

    lane = lax.broadcasted_iota(I32, (HG_SUB, c), 1)
    sub = lax.broadcasted_iota(I32, (HG_SUB, c), 0)
    rows = []
    for blk in range(c // HG_SUB):
        s0 = blk * HG_SUB
        bi = b[s0:s0 + HG_SUB]
        qi = q[s0:s0 + HG_SUB]
        ki = k[s0:s0 + HG_SUB]
        a_blk = jnp.zeros((HG_SUB, c), F32)
        has_off = blk > 0 if not reverse else blk < c // HG_SUB - 1
        if has_off:
            rr = s0 - 1 if not reverse else s0 + HG_SUB
            r = b[rr:rr + 1]
            qt = (qi * jnp.exp(bi - r)).astype(BF16)
            kt = (k * jnp.exp(jnp.minimum(r - b, 0.0))).astype(BF16)
            off = lax.dot_general(qt, kt, (((1,), (1,)), ((), ())), preferred_element_type=F32)
            keep = lane < s0 if not reverse else lane >= s0 + HG_SUB
            a_blk = jnp.where(keep, off, 0.0)
        for j in range(HG_SUB):
            dec = jnp.exp(jnp.minimum(bi - bi[j:j + 1], 0.0))
            col = jnp.sum(dec * qi * ki[j:j + 1], axis=1, keepdims=True)
            ok = (sub >= j) if not reverse else (sub <= j)
            a_blk = a_blk + jnp.where((lane == s0 + j) & ok, col, 0.0)
        rows.append(a_blk)
    a = jnp.concatenate(rows, axis=0).astype(BF16)
    return o + jnp.dot(a, vb, preferred_element_type=F32)


def _hg_fwd_kernel(q_ref, k_ref, v_ref, g_ref, tri_ref, o_ref, st_ref):
    @pl.when(pl.program_id(1) == 0)
    def _():
        st_ref[...] = jnp.zeros_like(st_ref)

    o_ref[...] = _hg_scan_body(False, q_ref, k_ref, v_ref, g_ref, tri_ref, st_ref)


def _hg_bwd_kernel(q_ref, k_ref, v_ref, g_ref, tri_ref, of_ref, og_ref, nw_ref, o_ref, st_ref):
    @pl.when(pl.program_id(1) == 0)
    def _():
        st_ref[...] = jnp.zeros_like(st_ref)

    o = of_ref[...] + _hg_scan_body(True, q_ref, k_ref, v_ref, g_ref, tri_ref, st_ref)
    ms = jnp.mean(o * o, axis=-1, keepdims=True)
    o_ref[...] = ((o * lax.rsqrt(ms + NORM_EPS) * nw_ref[...]) * og_ref[...]).astype(BF16)


def _hg_call(q, kf, gf, kb, gb, v, og, nw):
    s = q.shape[0]
    c = min(HG_CHUNK, s)
    n = s // c
    ii = np.arange(c)
    tri_f = jnp.asarray((ii[:, None] >= ii[None, :]).astype(np.float32))
    tri_b = jnp.asarray((ii[:, None] <= ii[None, :]).astype(np.float32))
    fwd = lambda h, i: (i, h)
    bwd = lambda h, i: (n - 1 - i, h)
    blk = lambda m: pl.BlockSpec((c, HG_DIM), m)
    scratch = [pltpu.VMEM((HG_DIM, HG_DIM), F32)]
    o_fwd = pl.pallas_call(
        _hg_fwd_kernel,
        grid=(HG_HEADS, n),
        in_specs=[blk(fwd)] * 4 + [_full((c, c))],
        out_specs=blk(fwd),
        out_shape=jax.ShapeDtypeStruct((s, HG_WIDTH), F32),
        scratch_shapes=scratch,
        compiler_params=_cparams("parallel", "arbitrary"),
        name="hg_scan_fwd",
    )(q, kf, v, gf, tri_f)
    return pl.pallas_call(
        _hg_bwd_kernel,
        grid=(HG_HEADS, n),
        in_specs=[blk(bwd)] * 4 + [_full((c, c)), blk(bwd), blk(bwd), _full((1, HG_DIM))],
        out_specs=blk(bwd),
        out_shape=jax.ShapeDtypeStruct((s, HG_WIDTH), BF16),
        scratch_shapes=scratch,
        compiler_params=_cparams("parallel", "arbitrary"),
        name="hg_scan_bwd",
    )(q, kb, v, gb, tri_b, o_fwd, og, nw)


def _merge_kernel(x_ref, oa_ref, oh_ref, g_ref, wa_ref, wh_ref, wo_ref, g1_ref, sh_ref, sc_ref,
                  nw_ref, wr_ref, xo_ref, h_ref, aff_ref):
    d = x_ref.shape[1]
    ya = jnp.dot(oa_ref[...], wa_ref[...], preferred_element_type=F32)
    yh = jnp.dot(oh_ref[...], wh_ref[...], preferred_element_type=F32)
    merged = g_ref[:, :d].astype(F32) * ya + g_ref[:, d:].astype(F32) * yh
    y = jnp.dot(merged.astype(BF16), wo_ref[...], preferred_element_type=F32)
    xn = x_ref[...] + g1_ref[...] * y
    xo_ref[...] = xn
    h = _rms_mod(xn, nw_ref[...], sc_ref[...], sh_ref[...])
    h_ref[...] = h.astype(BF16)
    logits = lax.dot_general(wr_ref[...], h, (((1,), (1,)), ((), ())), precision=HIGHEST,
                             preferred_element_type=F32)
    ex = jnp.exp(logits - jnp.max(logits, axis=0, keepdims=True))
    aff_ref[...] = ex / jnp.sum(ex, axis=0, keepdims=True)


def _merge_call(x, o_att, o_hg, g, w_ba, w_bh, w_out, gate1, shift2, scale2, nw2, w_router_t):
    s, d = x.shape
    t = min(TOKEN_BLOCK, s)
    row = lambda i: (i, 0)
    return pl.pallas_call(
        _merge_kernel,
        grid=(s // t,),
        in_specs=[
            pl.BlockSpec((t, d), row), pl.BlockSpec((t, ATT_WIDTH), row),
            pl.BlockSpec((t, HG_WIDTH), row), pl.BlockSpec((t, 2 * d), row),
            _full(w_ba.shape), _full(w_bh.shape), _full(w_out.shape),
            _full((1, d)), _full((1, d)), _full((1, d)), _full((1, d)), _full(w_router_t.shape),
        ],
        out_specs=[pl.BlockSpec((t, d), row), pl.BlockSpec((t, d), row),
                   pl.BlockSpec((N_EXPERTS, t), lambda i: (0, i))],
        out_shape=[jax.ShapeDtypeStruct((s, d), F32), jax.ShapeDtypeStruct((s, d), BF16),
                   jax.ShapeDtypeStruct((N_EXPERTS, s), F32)],
        compiler_params=_cparams("parallel"),
        name="merge_router",
    )(x, o_att, o_hg, g, w_ba, w_bh, w_out, gate1, shift2, scale2, nw2, w_router_t)


def _select_kernel(cap, aff_ref, tri_ref, pos_ref, starts_ref):
    n_exp, s = aff_ref.shape
    t = tri_ref.shape[0]
    n_blk = s // t
    aff = aff_ref[...]

    def bit_step(i, thr_bits):
        cand = thr_bits | lax.shift_left(jnp.int32(1), jnp.int32(30) - i)
        cnt = jnp.sum((aff >= pltpu.bitcast(cand, F32)).astype(F32), axis=1, keepdims=True)
        return jnp.where(cnt >= cap, cand, thr_bits)

    thr = pltpu.bitcast(lax.fori_loop(0, 31, bit_step, jnp.zeros((n_exp, 1), I32)), F32)
    n_gt = jnp.sum((aff > thr).astype(F32), axis=1, keepdims=True)
    need_eq = cap - n_gt
    tri = tri_ref[...]
    blk_lane = lax.broadcasted_iota(I32, starts_ref.shape, 1)

    starts_ref[...] = jnp.zeros(starts_ref.shape, I32)

    def blk_step(j, carry):
        c_eq, c_sel = carry
        start = pl.multiple_of(j * t, t)
        bt = aff_ref[:, pl.ds(start, t)]
        eq = bt == thr
        eq_f = eq.astype(BF16)
        eq_excl = c_eq + jnp.dot(eq_f, tri, preferred_element_type=F32) - eq_f.astype(F32)
        sel = (bt > thr) | (eq & (eq_excl < need_eq))
        sel_f = sel.astype(BF16)
        sel_incl = c_sel + jnp.dot(sel_f, tri, preferred_element_type=F32)
        pos_ref[:, pl.ds(start, t)] = jnp.where(sel, sel_incl - 1.0, -1.0).astype(I32)
        starts_ref[...] = jnp.where(blk_lane == j, c_sel.astype(I32), starts_ref[...])
        return (c_eq + jnp.sum(eq_f.astype(F32), axis=1, keepdims=True),
                c_sel + jnp.sum(sel_f.astype(F32), axis=1, keepdims=True))

    zero = jnp.zeros((n_exp, 1), F32)
    _, c_sel = lax.fori_loop(0, n_blk, blk_step, (zero, zero))
    starts_ref[...] = jnp.where(blk_lane >= n_blk, c_sel.astype(I32), starts_ref[...])


def _select_call(aff_t, cap):
    n_exp, s = aff_t.shape
    t = min(ROUTE_BLOCK, s)
    ii = np.arange(t)
    tri = jnp.asarray((ii[:, None] <= ii[None, :]).astype(np.float32), dtype=BF16)
    return pl.pallas_call(
        functools.partial(_select_kernel, float(cap)),
        grid=(1,),
        in_specs=[_full((n_exp, s)), _full((t, t))],
        out_specs=[_full((n_exp, s)), _full((n_exp, LANES))],
        out_shape=[jax.ShapeDtypeStruct((n_exp, s), I32), jax.ShapeDtypeStruct((n_exp, LANES), I32)],
        compiler_params=_cparams("arbitrary"),
        name="route_select",
    )(aff_t, tri)


def _gather_kernel(cap, per, starts_ref, pos_ref, h_ref, xe_ref, acc_ref):
    e = pl.program_id(0)
    b = pl.program_id(1)
    t = h_ref.shape[0]

    @pl.when(b == 0)
    def _():
        acc_ref[...] = jnp.zeros_like(acc_ref)

    lo = starts_ref[e, b * per]
    hi = starts_ref[e, (b + 1) * per]
    c0 = lo // ROW_CHUNK
    pos = pos_ref[pl.ds(e, 1), :]
    slot = lax.broadcasted_iota(I32, (ROW_CHUNK, t), 0)
    for kk in range(t // ROW_CHUNK + 1):
        c = c0 + kk

        @pl.when((hi > lo) & (c * ROW_CHUNK < hi) & (c < cap // ROW_CHUNK))
        def _():
            base = pl.multiple_of(c * ROW_CHUNK, ROW_CHUNK)
            onehot = (pos - base == slot).astype(BF16)
            acc_ref[pl.ds(base, ROW_CHUNK), :] += jnp.dot(onehot, h_ref[...],
                                                          preferred_element_type=F32)

    @pl.when(b == pl.num_programs(1) - 1)
    def _():
        xe_ref[0] = acc_ref[...].astype(BF16)


def _gather_call(starts, pos_t, h, cap):
    s, d = h.shape
    t = min(GATHER_BLOCK, s)
    return pl.pallas_call(
        functools.partial(_gather_kernel, cap, t // min(ROUTE_BLOCK, s)),
        grid_spec=pltpu.PrefetchScalarGridSpec(
            num_scalar_prefetch=1,
            grid=(N_EXPERTS, s // t),
            in_specs=[pl.BlockSpec((N_EXPERTS, t), lambda e, b, st: (0, b)),
                      pl.BlockSpec((t, d), lambda e, b, st: (b, 0))],
            out_specs=pl.BlockSpec((1, cap, d), lambda e, b, st: (e, 0, 0)),
            scratch_shapes=[pltpu.VMEM((cap, d), F32)],
        ),
        out_shape=jax.ShapeDtypeStruct((N_EXPERTS, cap, d), BF16),
        compiler_params=_cparams("parallel", "arbitrary"),
        name="expert_gather",
    )(starts, pos_t, h)


def _ffn_kernel(xe_ref, wg_ref, wu_ref, wd_ref, ye_ref, acc_ref):
    f = pl.program_id(1)
    x = xe_ref[0]
    a = jnp.dot(x, wg_ref[0, 0].astype(BF16), preferred_element_type=F32)
    u = jnp.dot(x, wu_ref[0, 0].astype(BF16), preferred_element_type=F32)
    hid = (_silu(a) * u).astype(BF16)
    contrib = jnp.dot(hid, wd_ref[0, 0].astype(BF16), preferred_element_type=F32)

    @pl.when(f == 0)
    def _():
        acc_ref[...] = contrib

    @pl.when(f > 0)
    def _():
        acc_ref[...] += contrib

    @pl.when(f == pl.num_programs(1) - 1)
    def _():
        ye_ref[0] = acc_ref[...].astype(BF16)


def _ffn_call(layer, xe, w_gate, w_up, w_down):
    n_exp, cap, d = xe.shape
    ff = w_gate.shape[3]
    fb = min(FF_BLOCK, ff)
    return pl.pallas_call(
        _ffn_kernel,
        grid=(n_exp, ff // fb),
        in_specs=[pl.BlockSpec((1, cap, d), lambda e, f: (e, 0, 0)),
                  pl.BlockSpec((1, 1, d, fb), lambda e, f: (layer, e, 0, f)),
                  pl.BlockSpec((1, 1, d, fb), lambda e, f: (layer, e, 0, f)),
                  pl.BlockSpec((1, 1, fb, d), lambda e, f: (layer, e, f, 0))],
        out_specs=pl.BlockSpec((1, cap, d), lambda e, f: (e, 0, 0)),
        out_shape=jax.ShapeDtypeStruct((n_exp, cap, d), BF16),
        scratch_shapes=[pltpu.VMEM((cap, d), F32)],
        compiler_params=_cparams("parallel", "arbitrary"),
        name="expert_ffn",
    )(xe, w_gate, w_up, w_down)


def _combine_kernel(cap, n_cand, starts_ref, pos_ref, w_ref, x_ref, g2_ref, *rest):
    ye_refs = rest[:n_cand]
    xo_ref, acc_ref = rest[n_cand:]
    b = pl.program_id(0)
    e = pl.program_id(1)
    t = x_ref.shape[0]

    @pl.when(e == 0)
    def _():
        acc_ref[...] = jnp.zeros_like(acc_ref)

    lo = starts_ref[e, b]
    hi = starts_ref[e, b + 1]
    c0 = lo // ROW_CHUNK
    exp_lane = lax.broadcasted_iota(I32, pos_ref.shape, 1) == e
    pos = jnp.sum(jnp.where(exp_lane, pos_ref[...], 0), axis=1, keepdims=True)
    wgt = jnp.sum(jnp.where(exp_lane, w_ref[...], 0.0), axis=1, keepdims=True)
    slot = lax.broadcasted_iota(I32, (t, ROW_CHUNK), 1)
    for kk in range(n_cand):
        c = c0 + kk

        @pl.when((hi > lo) & (c * ROW_CHUNK < hi) & (c < cap // ROW_CHUNK))
        def _():
            onehot = (pos - c * ROW_CHUNK == slot).astype(BF16)
            acc_ref[...] += wgt * jnp.dot(onehot, ye_refs[kk][0], preferred_element_type=F32)

    @pl.when(e == pl.num_programs(1) - 1)
    def _():
        xo_ref[...] = x_ref[...] + g2_ref[...] * acc_ref[...]


def _combine_call(starts, pos_tok, w_tok, x, gate2, ye):
    s, d = x.shape
    n_exp, cap, _ = ye.shape
    t = min(ROUTE_BLOCK, s)
    n_chunks = cap // ROW_CHUNK
    n_cand = min(t // ROW_CHUNK + 1, n_chunks)

    def ye_spec(kk):
        return pl.BlockSpec(
            (1, ROW_CHUNK, d),
            lambda b, e, st: (e, jnp.minimum(st[e, b] // ROW_CHUNK + kk, n_chunks - 1), 0))

    return pl.pallas_call(
        functools.partial(_combine_kernel, cap, n_cand),
        grid_spec=pltpu.PrefetchScalarGridSpec(
            num_scalar_prefetch=1,
            grid=(s // t, n_exp),
            in_specs=[pl.BlockSpec((t, n_exp), lambda b, e, st: (b, 0)),
                      pl.BlockSpec((t, n_exp), lambda b, e, st: (b, 0)),
                      pl.BlockSpec((t, d), lambda b, e, st: (b, 0)),
                      pl.BlockSpec((1, d), lambda b, e, st: (0, 0))]
            + [ye_spec(kk) for kk in range(n_cand)],
            out_specs=pl.BlockSpec((t, d), lambda b, e, st: (b, 0)),
            scratch_shapes=[pltpu.VMEM((t, d), F32)],
        ),
        out_shape=jax.ShapeDtypeStruct((s, d), F32),
        compiler_params=_cparams("parallel", "arbitrary"),
        name="expert_combine",
    )(starts, pos_tok, w_tok, x, gate2, *([ye] * n_cand))


def _rope_tables(seq_len):
    rows = seq_len // GRID_W
    row_id = jnp.repeat(jnp.arange(rows, dtype=F32), GRID_W)
    col_id = jnp.tile(jnp.arange(GRID_W, dtype=F32), rows)
    inv_freq = ROPE_THETA ** (-jnp.arange(0, ROPE_AXIS_DIM, 2, dtype=F32) / ROPE_AXIS_DIM)
    ang = jnp.stack([row_id[:, None] * inv_freq, col_id[:, None] * inv_freq], axis=1)
    cos, sin = jnp.cos(ang), jnp.sin(ang)
    cos_h = jnp.concatenate([cos, cos], axis=-1).reshape(seq_len, ATT_HEAD_DIM)
    sin_h = jnp.concatenate([-sin, sin], axis=-1).reshape(seq_len, ATT_HEAD_DIM)
    reps = LANES // ATT_HEAD_DIM
    return jnp.tile(cos_h, (1, reps)), jnp.tile(sin_h, (1, reps))


def _head_mean_operator():
    ii = np.arange(ATT_WIDTH)
    same = (ii[:, None] // ATT_HEAD_DIM) == (ii[None, :] // ATT_HEAD_DIM)
    return jnp.asarray(same.astype(np.float32) / ATT_HEAD_DIM)


def kernel(x, c, ada_w, ada_b, norm_mix_w, norm_ffn_w, w_in, q_norm_w, k_norm_w, hg_lower_bounds,
           hg_norm_w, w_branch_att, w_branch_hg, w_out, w_router, w_exp_gate, w_exp_up, w_exp_down):
    batch, s, d = x.shape
    assert batch == 1, "kernel is written for a single sequence"
    n_layers = ada_w.shape[0]
    cap = EC_CAPACITY_FACTOR * s // N_EXPERTS
    assert cap % ROW_CHUNK == 0 and s % min(ROUTE_BLOCK, s) == 0

    cos, sin = _rope_tables(s)
    bd = _head_mean_operator()
    mods = _ada_mod(c, ada_w, ada_b)
    qkv_w = ATT_WIDTH + 2 * ATT_KV_WIDTH
    hg_w = 5 * HG_WIDTH
    xs = x[0]
    for l in range(n_layers):
        shift1, scale1, gate1, shift2, scale2, gate2 = [
            mods[l, i * d:(i + 1) * d].reshape(1, d) for i in range(6)]
        qk_w = ATT_WIDTH + ATT_KV_WIDTH
        h, q, k, v = _qkv_call(
            xs, shift1, scale1, norm_mix_w[l].reshape(1, d), w_in[l, :, :qk_w].astype(BF16),
            w_in[l, :, qk_w:qkv_w].T.astype(BF16),
            jnp.tile(q_norm_w[l], ATT_HEADS).reshape(1, ATT_WIDTH),
            jnp.tile(k_norm_w[l], ATT_KV_HEADS).reshape(1, ATT_KV_WIDTH), cos, sin, bd)
        hq, kf, gf, kb, gb, hv, og = _hg_proj_call(
            l, h, w_in[l, :, qkv_w:qkv_w + hg_w].astype(BF16), hg_lower_bounds)
        g = _gate_call(h, w_in[l, :, qkv_w + hg_w:].astype(BF16))
        o_att = _attn_call(q, k, v)
        o_hg = _hg_call(hq, kf, gf, kb, gb, hv, og, hg_norm_w[l].reshape(1, HG_DIM))
        xs, h2, aff_t = _merge_call(
            xs, o_att, o_hg, g, w_branch_att[l].astype(BF16), w_branch_hg[l].astype(BF16),
            w_out[l].astype(BF16), gate1, shift2, scale2, norm_ffn_w[l].reshape(1, d),
            w_router[l].T)
        pos_t, starts = _select_call(aff_t, cap)
        xe = _gather_call(starts, pos_t, h2, cap)
        ye = _ffn_call(l, xe, w_exp_gate, w_exp_up, w_exp_down)
        xs = _combine_call(starts, pos_t.T, aff_t.T, xs, gate2, ye)
    return xs[None]
```

```python
import functools

import numpy as np
import jax
import jax.numpy as jnp
from jax import lax
from jax.experimental import pallas as pl
from jax.experimental.pallas import tpu as pltpu

F32 = jnp.float32
BF16 = jnp.bfloat16
I32 = jnp.int32
HIGHEST = lax.Precision.HIGHEST

GRID_W = 64
ATT_HEADS = 8
ATT_KV_HEADS = 2
ATT_GROUP = ATT_HEADS // ATT_KV_HEADS
ATT_HEAD_DIM = 64
ATT_WIDTH = ATT_HEADS * ATT_HEAD_DIM
ATT_KV_WIDTH = ATT_KV_HEADS * ATT_HEAD_DIM
ROPE_AXIS_DIM = ATT_HEAD_DIM // 2
ROPE_THETA = 10000.0
HG_HEADS = 4
HG_DIM = 128
HG_WIDTH = HG_HEADS * HG_DIM
MIN_FORGET = 1e-6
N_EXPERTS = 16
EC_CAPACITY_FACTOR = 2
NORM_EPS = 1e-6
LOG2_E = 1.4426950408889634

LANES = 128
VMEM_LIMIT_BYTES = 56 * 1024 * 1024

TOKEN_BLOCK = 512
ATT_Q_BLOCK = 512
ATT_KV_BLOCK = 512
ATT_ROW_PARTS = 4
ATT_ONES_ROWS = 16
HG_CHUNK = 128
HG_SUB = 16
ROUTE_BLOCK = 512
GATHER_BLOCK = 1024
ROW_CHUNK = 256
FF_BLOCK = 512


def _cparams(*sem):
    return pltpu.CompilerParams(dimension_semantics=sem, vmem_limit_bytes=VMEM_LIMIT_BYTES)


def _sigmoid(x):
    return 1.0 / (1.0 + jnp.exp(-x))


def _silu(x):
    return x * _sigmoid(x)


def _full(shape):
    return pl.BlockSpec(shape, lambda *_: (0,) * len(shape))


def _ada_kernel(c_ref, w_ref, b_ref, o_ref):
    ca = _silu(c_ref[...])
    o_ref[0] = jnp.dot(ca, w_ref[0], precision=HIGHEST, preferred_element_type=F32) + b_ref[0]


def _ada_mod(c, ada_w, ada_b):
    n_layers, d, w6 = ada_w.shape
    col = 1536
    c8 = jnp.broadcast_to(c, (8, d))
    out = pl.pallas_call(
        _ada_kernel,
        grid=(n_layers, w6 // col),
        in_specs=[
            pl.BlockSpec((8, d), lambda l, j: (0, 0)),
            pl.BlockSpec((1, d, col), lambda l, j: (l, 0, j)),
            pl.BlockSpec((1, 1, col), lambda l, j: (l, 0, j)),
        ],
        out_specs=pl.BlockSpec((1, 8, col), lambda l, j: (l, 0, j)),
        out_shape=jax.ShapeDtypeStruct((n_layers, 8, w6), F32),
        compiler_params=_cparams("parallel", "parallel"),
        name="ada_mod",
    )(c8, ada_w, ada_b.reshape(n_layers, 1, w6))
    return out[:, 0, :]


def _rms_mod(x, nw, scale, shift):
    ms = jnp.mean(x * x, axis=-1, keepdims=True)
    return (x * lax.rsqrt(ms + NORM_EPS) * nw) * (1.0 + scale) + shift


def _qkv_kernel(x_ref, sh_ref, sc_ref, nw_ref, w_ref, wvt_ref, qnw_ref, knw_ref, cos_ref,
                sin_ref, bd_ref, h_ref, q_ref, k_ref, vt_ref):
    h = _rms_mod(x_ref[...], nw_ref[...], sc_ref[...], sh_ref[...])
    hb = h.astype(BF16)
    h_ref[...] = hb
    proj = jnp.dot(hb, w_ref[...], preferred_element_type=F32)
    aq = proj[:, :ATT_WIDTH]
    ak = proj[:, ATT_WIDTH:]
    avt = lax.dot_general(wvt_ref[...], hb, (((1,), (1,)), ((), ())),
                          preferred_element_type=F32)
    bd = bd_ref[...]
    qms = jnp.dot(aq * aq, bd, precision=HIGHEST, preferred_element_type=F32)
    kms = jnp.dot(ak * ak, bd[:ATT_KV_WIDTH, :ATT_KV_WIDTH], precision=HIGHEST,
                  preferred_element_type=F32)
    qn = aq * lax.rsqrt(qms + NORM_EPS) * qnw_ref[...]
    kn = ak * lax.rsqrt(kms + NORM_EPS) * knw_ref[...]
    cos = cos_ref[...]
    sin = sin_ref[...]
    lane = lax.broadcasted_iota(I32, cos.shape, 1)
    first_half = (lane % ROPE_AXIS_DIM) < (ROPE_AXIS_DIM // 2)

    def rope(xs):
        up = pltpu.roll(xs, LANES - ROPE_AXIS_DIM // 2, 1)
        dn = pltpu.roll(xs, ROPE_AXIS_DIM // 2, 1)
        return xs * cos + jnp.where(first_half, up, dn) * sin

    q_scale = ATT_HEAD_DIM ** -0.5 * LOG2_E
    for j in range(ATT_WIDTH // LANES):
        qr = rope(qn[:, j * LANES:(j + 1) * LANES]) * q_scale
        q_ref[2 * j] = qr[:, :ATT_HEAD_DIM].astype(BF16)
        q_ref[2 * j + 1] = qr[:, ATT_HEAD_DIM:].astype(BF16)
    kr = rope(kn)
    ones = jnp.ones((ATT_ONES_ROWS, avt.shape[1]), F32)
    for j in range(ATT_KV_HEADS):
        k_ref[j] = kr[:, j * ATT_HEAD_DIM:(j + 1) * ATT_HEAD_DIM].astype(BF16)
        vj = avt[j * ATT_HEAD_DIM:(j + 1) * ATT_HEAD_DIM]
        vt_ref[j] = jnp.concatenate([vj, ones], axis=0).astype(BF16)


def _qkv_call(x, shift, scale, nw, w_qk, w_v_t, qnw, knw, cos, sin, bd):
    s, d = x.shape
    t = min(TOKEN_BLOCK, s)
    vr = ATT_HEAD_DIM + ATT_ONES_ROWS
    row = lambda i: (i, 0)
    head = lambda i: (0, i, 0)
    return pl.pallas_call(
        _qkv_kernel,
        grid=(s // t,),
        in_specs=[
            pl.BlockSpec((t, d), row), _full((1, d)), _full((1, d)), _full((1, d)),
            _full(w_qk.shape), _full(w_v_t.shape), _full((1, ATT_WIDTH)), _full((1, ATT_KV_WIDTH)),
            pl.BlockSpec((t, LANES), row), pl.BlockSpec((t, LANES), row), _full(bd.shape),
        ],
        out_specs=[
            pl.BlockSpec((t, d), row),
            pl.BlockSpec((ATT_HEADS, t, ATT_HEAD_DIM), head),
            pl.BlockSpec((ATT_KV_HEADS, t, ATT_HEAD_DIM), head),
            pl.BlockSpec((ATT_KV_HEADS, vr, t), lambda i: (0, 0, i)),
        ],
        out_shape=[
            jax.ShapeDtypeStruct((s, d), BF16),
            jax.ShapeDtypeStruct((ATT_HEADS, s, ATT_HEAD_DIM), BF16),
            jax.ShapeDtypeStruct((ATT_KV_HEADS, s, ATT_HEAD_DIM), BF16),
            jax.ShapeDtypeStruct((ATT_KV_HEADS, vr, s), BF16),
        ],
        compiler_params=_cparams("parallel"),
        name="qkv_proj",
    )(x, shift, scale, nw, w_qk, w_v_t, qnw, knw, cos, sin, bd)


def _hg_proj_kernel(layer, h_ref, w_ref, lb_ref, q_ref, kf_ref, gf_ref, kb_ref, gb_ref,
                    v_ref, og_ref):
    hb = h_ref[...]

    def proj(j):
        return jnp.dot(hb, w_ref[:, j * HG_WIDTH:(j + 1) * HG_WIDTH], preferred_element_type=F32)

    lb_raw = lb_ref[...]
    n_layers = lb_raw.shape[0]
    mx = lb_raw[0]
    for i in range(1, n_layers):
        mx = jnp.maximum(mx, lb_raw[i])
    ex = [jnp.exp(lb_raw[i] - mx) for i in range(n_layers)]
    den = ex[0]
    for i in range(1, n_layers):
        den = den + ex[i]
    lb = jnp.zeros_like(mx)
    for i in range(1, layer + 1):
        lb = lb + ex[i] / den

    def forget(z, lbd):
        f = lbd + (1.0 - lbd) * _sigmoid(z)
        return jnp.log(jnp.maximum(f, MIN_FORGET)), (1.0 - lbd) * _sigmoid(-z)

    q_ref[...] = _silu(proj(0))
    gf, kf = forget(proj(1), lb[0:1])
    gf_ref[...] = gf
    kf_ref[...] = kf
    gb, kb = forget(proj(2), lb[1:2])
    gb_ref[...] = gb
    kb_ref[...] = kb
    v_ref[...] = proj(3)
    og_ref[...] = _silu(proj(4))


def _hg_proj_call(layer, h, w_hg, lb_raw):
    s, d = h.shape
    t = min(TOKEN_BLOCK, s)
    row = lambda i: (i, 0)
    out = jax.ShapeDtypeStruct((s, HG_WIDTH), F32)
    return pl.pallas_call(
        functools.partial(_hg_proj_kernel, layer),
        grid=(s // t,),
        in_specs=[pl.BlockSpec((t, d), row), _full(w_hg.shape), _full(lb_raw.shape)],
        out_specs=[pl.BlockSpec((t, HG_WIDTH), row)] * 7,
        out_shape=[out] * 7,
        compiler_params=_cparams("parallel"),
        name="hg_proj",
    )(h, w_hg, lb_raw)


def _gate_kernel(h_ref, w_ref, g_ref):
    g_ref[...] = _sigmoid(jnp.dot(h_ref[...], w_ref[...], preferred_element_type=F32)).astype(BF16)


def _gate_call(h, w_gate):
    s, d = h.shape
    t = min(TOKEN_BLOCK, s)
    n = w_gate.shape[1]
    return pl.pallas_call(
        _gate_kernel,
        grid=(s // t,),
        in_specs=[pl.BlockSpec((t, d), lambda i: (i, 0)), _full(w_gate.shape)],
        out_specs=pl.BlockSpec((t, n), lambda i: (i, 0)),
        out_shape=jax.ShapeDtypeStruct((s, n), BF16),
        compiler_params=_cparams("parallel"),
        name="gate_proj",
    )(h, w_gate)


def _attn_kernel(q_ref, k_ref, vt_ref, o_ref, m_ref, acc_ref):
    g, tq, hd = q_ref.shape
    s = k_ref.shape[1]
    tk = min(ATT_KV_BLOCK, s)
    heads_per_part = g // ATT_ROW_PARTS
    cols = heads_per_part * tq
    m_ref[...] = jnp.full(m_ref.shape, -jnp.inf, F32)
    acc_ref[...] = jnp.zeros(acc_ref.shape, F32)

    def body(j, carry):
        start = pl.multiple_of(j * tk, tk)
        ks = k_ref[0, pl.ds(start, tk), :]
        vt = vt_ref[0, :, pl.ds(start, tk)]
        sts = []
        for r in range(ATT_ROW_PARTS):
            q = q_ref[r * heads_per_part:(r + 1) * heads_per_part].reshape(cols, hd)
            sts.append(lax.dot_general(ks, q, (((1,), (1,)), ((), ())),
                                       preferred_element_type=F32))
        for r in range(ATT_ROW_PARTS):
            sl = pl.ds(r * cols, cols)
            st = sts[r]
            m_old = m_ref[:, sl]
            m_new = jnp.maximum(m_old, jnp.max(st, axis=0, keepdims=True))
            p = jnp.exp2(st - m_new).astype(BF16)
            acc_ref[:, sl] = (jnp.exp2(m_old - m_new) * acc_ref[:, sl]
                              + jnp.dot(vt, p, preferred_element_type=F32))
            m_ref[:, sl] = m_new
        return carry

    lax.fori_loop(0, s // tk, body, 0)
    acc = acc_ref[...]
    o_t = acc[:hd] / acc[hd:hd + 1]
    for i in range(g):
        o_ref[:, i * hd:(i + 1) * hd] = o_t[:, i * tq:(i + 1) * tq].T.astype(BF16)


def _attn_call(q, k, vt):
    _, s, hd = q.shape
    vr = vt.shape[1]
    tq = min(ATT_Q_BLOCK, s)
    return pl.pallas_call(
        _attn_kernel,
        grid=(ATT_KV_HEADS, s // tq),
        in_specs=[
            pl.BlockSpec((ATT_GROUP, tq, hd), lambda h, i: (h, i, 0)),
            pl.BlockSpec((1, s, hd), lambda h, i: (h, 0, 0)),
            pl.BlockSpec((1, vr, s), lambda h, i: (h, 0, 0)),
        ],
        out_specs=pl.BlockSpec((tq, ATT_GROUP * hd), lambda h, i: (i, h)),
        out_shape=jax.ShapeDtypeStruct((s, ATT_HEADS * hd), BF16),
        scratch_shapes=[pltpu.VMEM((1, ATT_GROUP * tq), F32),
                        pltpu.VMEM((vr, ATT_GROUP * tq), F32)],
        compiler_params=_cparams("parallel", "parallel"),
        name="attention",
    )(q, k, vt)


def _chunk_cumsum(g, reverse):
    c = g.shape[0]
    row = lax.broadcasted_iota(I32, g.shape, 0)
    x = g
    step = 1
    while step < c:
        if not reverse:
            x = x + jnp.where(row >= step, pltpu.roll(x, step, 0), 0.0)
        else:
            x = x + jnp.where(row < c - step, pltpu.roll(x, c - step, 0), 0.0)
        step *= 2
    return x


def _boundary_rows(b_ref, half, idx):
    c, dd = b_ref.shape
    blk = 2 * half
    pieces = []
    if blk >= 8:
        for s0 in range(0, c, blk):
            pieces.append(jnp.broadcast_to(b_ref[pl.ds(s0 + idx, 1), :], (blk, dd)))
    else:
        sub = lax.broadcasted_iota(I32, (8, dd), 0)
        for v0 in range(0, c, 8):
            piece = jnp.broadcast_to(b_ref[pl.ds(v0 + idx, 1), :], (8, dd))
            for s0 in range(v0 + blk, v0 + 8, blk):
                row = jnp.broadcast_to(b_ref[pl.ds(s0 + idx, 1), :], (8, dd))
                piece = jnp.where(sub >= s0 - v0, row, piece)
            pieces.append(piece)
    return jnp.concatenate(pieces, axis=0)


def _hg_scan_body(reverse, q_ref, k_ref, v_ref, g_ref, mask_ref, st_ref, b_ref):
    c = q_ref.shape[0]
    heads = [slice(h * HG_DIM, (h + 1) * HG_DIM) for h in range(q_ref.shape[1] // HG_DIM)]
    nt = (((1,), (1,)), ((), ()))
    q = q_ref[...]
    k = k_ref[...]
    vb = v_ref[...].astype(BF16)
    b = _chunk_cumsum(g_ref[...] * LOG2_E, reverse)
    b_ref[...] = b
    b_edge = b_ref[pl.ds(c - 1 if not reverse else 0, 1), :]
    qh = (q * jnp.exp2(b)).astype(BF16)
    kh = (k * jnp.exp2(b_edge - b)).astype(BF16)
    st_decay = jnp.exp2(b_edge)
    outs = []
    for h, sl in enumerate(heads):
        st = st_ref[h]
        outs.append(lax.dot_general(qh[:, sl], st.astype(BF16), nt, preferred_element_type=F32))
        st_ref[h] = st * st_decay[:, sl] + lax.dot_general(
            vb[:, sl], kh[:, sl], (((0,), (0,)), ((), ())), preferred_element_type=F32)

    qb = q.astype(BF16)
    kb = k.astype(BF16)
    scores = [mask_ref[0] * lax.dot_general(qb[:, sl], kb[:, sl], nt, preferred_element_type=F32)
              for sl in heads]
    half = c // 2
    level = 1
    while half >= 1:
        d = b - _boundary_rows(b_ref, half, half - 1 if not reverse else half)
        qt = (q * jnp.exp2(jnp.minimum(d, 0.0))).astype(BF16)
        kt = (k * jnp.exp2(jnp.minimum(-d, 0.0))).astype(BF16)
        mask = mask_ref[level]
        scores = [a + mask * lax.dot_general(qt[:, sl], kt[:, sl], nt, preferred_element_type=F32)
                  for a, sl in zip(scores, heads)]
        half //= 2
        level += 1
    return [o + jnp.dot(a.astype(BF16), vb[:, sl], preferred_element_type=F32)
            for o, a, sl in zip(outs, scores, heads)]


def _hg_fwd_kernel(q_ref, k_ref, v_ref, g_ref, mask_ref, o_ref, st_ref, b_ref):
    @pl.when(pl.program_id(0) == 0)
    def _():
        st_ref[...] = jnp.zeros_like(st_ref)

    outs = _hg_scan_body(False, q_ref, k_ref, v_ref, g_ref, mask_ref, st_ref, b_ref)
    o_ref[...] = jnp.concatenate(outs, axis=1)


def _hg_bwd_kernel(q_ref, k_ref, v_ref, g_ref, mask_ref, of_ref, og_ref, nw_ref, o_ref, st_ref,
                   b_ref):
    @pl.when(pl.program_id(0) == 0)
    def _():
        st_ref[...] = jnp.zeros_like(st_ref)

    outs = _hg_scan_body(True, q_ref, k_ref, v_ref, g_ref, mask_ref, st_ref, b_ref)
    normed = []
    for h, o_b in enumerate(outs):
        o = of_ref[:, h * HG_DIM:(h + 1) * HG_DIM] + o_b
        ms = jnp.mean(o * o, axis=-1, keepdims=True)
        normed.append(o * lax.rsqrt(ms + NORM_EPS) * nw_ref[...])
    o_ref[...] = (jnp.concatenate(normed, axis=1) * og_ref[...]).astype(BF16)


def _level_masks(c, reverse):
    ii = np.arange(c)[:, None]
    jj = np.arange(c)[None, :]
    masks = [ii == jj]
    half = c // 2
    while half >= 1:
        blk = 2 * half
        same = (ii // blk) == (jj // blk)
        q_hi, k_hi = (ii % blk) >= half, (jj % blk) >= half
        masks.append(same & (q_hi & ~k_hi if not reverse else ~q_hi & k_hi))
        half //= 2
    return jnp.asarray(np.stack(masks).astype(np.float32))


def _hg_call(q, kf, gf, kb, gb, v, og, nw):
    s = q.shape[0]
    c = min(HG_CHUNK, s)
    n = s // c
    mask_f = _level_masks(c, False)
    mask_b = _level_masks(c, True)
    fwd = lambda i: (i, 0)
    bwd = lambda i: (n - 1 - i, 0)
    blk = lambda m: pl.BlockSpec((c, HG_WIDTH), m)
    scratch = [pltpu.VMEM((HG_HEADS, HG_DIM, HG_DIM), F32), pltpu.VMEM((c, HG_WIDTH), F32)]
    o_fwd = pl.pallas_call(
        _hg_fwd_kernel,
        grid=(n,),
        in_specs=[blk(fwd)] * 4 + [_full(mask_f.shape)],
        out_specs=blk(fwd),
        out_shape=jax.ShapeDtypeStruct((s, HG_WIDTH), F32),
        scratch_shapes=scratch,
        compiler_params=_cparams("arbitrary"),
        name="hg_scan_fwd",
    )(q, kf, v, gf, mask_f)
    return pl.pallas_call(
        _hg_bwd_kernel,
        grid=(n,),
        in_specs=[blk(bwd)] * 4 + [_full(mask_b.shape), blk(bwd), blk(bwd), _full((1, HG_DIM))],
        out_specs=blk(bwd),
        out_shape=jax.ShapeDtypeStruct((s, HG_WIDTH), BF16),
        scratch_shapes=scratch,
        compiler_params=_cparams("arbitrary"),
        name="hg_scan_bwd",
    )(q, kb, v, gb, mask_b, o_fwd, og, nw)


def _merge_kernel(x_ref, oa_ref, oh_ref, g_ref, wa_ref, wh_ref, wo_ref, g1_ref, sh_ref, sc_ref,
                  nw_ref, wr_ref, xo_ref, h_ref, aff_ref):
    d = x_ref.shape[1]
    ya = jnp.dot(oa_ref[...], wa_ref[...], preferred_element_type=F32)
    yh = jnp.dot(oh_ref[...], wh_ref[...], preferred_element_type=F32)
    merged = g_ref[:, :d].astype(F32) * ya + g_ref[:, d:].astype(F32) * yh
    y = jnp.dot(merged.astype(BF16), wo_ref[...], preferred_element_type=F32)
    xn = x_ref[...] + g1_ref[...] * y
    xo_ref[...] = xn
    h = _rms_mod(xn, nw_ref[...], sc_ref[...], sh_ref[...])
    h_ref[...] = h.astype(BF16)
    logits = lax.dot_general(wr_ref[...], h, (((1,), (1,)), ((), ())), precision=HIGHEST,
                             preferred_element_type=F32)
    ex = jnp.exp(logits - jnp.max(logits, axis=0, keepdims=True))
    aff_ref[...] = ex / jnp.sum(ex, axis=0, keepdims=True)


def _merge_call(x, o_att, o_hg, g, w_ba, w_bh, w_out, gate1, shift2, scale2, nw2, w_router_t):
    s, d = x.shape
    t = min(TOKEN_BLOCK, s)
    row = lambda i: (i, 0)
    return pl.pallas_call(
        _merge_kernel,
        grid=(s // t,),
        in_specs=[
            pl.BlockSpec((t, d), row), pl.BlockSpec((t, ATT_WIDTH), row),
            pl.BlockSpec((t, HG_WIDTH), row), pl.BlockSpec((t, 2 * d), row),
            _full(w_ba.shape), _full(w_bh.shape), _full(w_out.shape),
            _full((1, d)), _full((1, d)), _full((1, d)), _full((1, d)), _full(w_router_t.shape),
        ],
        out_specs=[pl.BlockSpec((t, d), row), pl.BlockSpec((t, d), row),
                   pl.BlockSpec((N_EXPERTS, t), lambda i: (0, i))],
        out_shape=[jax.ShapeDtypeStruct((s, d), F32), jax.ShapeDtypeStruct((s, d), BF16),
                   jax.ShapeDtypeStruct((N_EXPERTS, s), F32)],
        compiler_params=_cparams("parallel"),
        name="merge_router",
    )(x, o_att, o_hg, g, w_ba, w_bh, w_out, gate1, shift2, scale2, nw2, w_router_t)


def _select_kernel(cap, aff_ref, tri_ref, pos_ref, starts_ref):
    n_exp, s = aff_ref.shape
    t = tri_ref.shape[0]
    n_blk = s // t
    aff = aff_ref[...]

    def bit_step(i, thr_bits):
        cand = thr_bits | lax.shift_left(jnp.int32(1), jnp.int32(30) - i)
        cnt = jnp.sum((aff >= pltpu.bitcast(cand, F32)).astype(F32), axis=1, keepdims=True)
        return jnp.where(cnt >= cap, cand, thr_bits)

    thr = pltpu.bitcast(lax.fori_loop(0, 31, bit_step, jnp.zeros((n_exp, 1), I32)), F32)
    n_gt = jnp.sum((aff > thr).astype(F32), axis=1, keepdims=True)
    need_eq = cap - n_gt
    tri = tri_ref[...]
    blk_lane = lax.broadcasted_iota(I32, starts_ref.shape, 1)

    starts_ref[...] = jnp.zeros(starts_ref.shape, I32)

    def blk_step(j, carry):
        c_eq, c_sel = carry
        start = pl.multiple_of(j * t, t)
        bt = aff_ref[:, pl.ds(start, t)]
        eq = bt == thr
        eq_f = eq.astype(BF16)
        eq_excl = c_eq + jnp.dot(eq_f, tri, preferred_element_type=F32) - eq_f.astype(F32)
        sel = (bt > thr) | (eq & (eq_excl < need_eq))
        sel_f = sel.astype(BF16)
        sel_incl = c_sel + jnp.dot(sel_f, tri, preferred_element_type=F32)
        pos_ref[:, pl.ds(start, t)] = jnp.where(sel, sel_incl - 1.0, -1.0).astype(I32)
        starts_ref[...] = jnp.where(blk_lane == j, c_sel.astype(I32), starts_ref[...])
        return (c_eq + jnp.sum(eq_f.astype(F32), axis=1, keepdims=True),
                c_sel + jnp.sum(sel_f.astype(F32), axis=1, keepdims=True))

    zero = jnp.zeros((n_exp, 1), F32)
    _, c_sel = lax.fori_loop(0, n_blk, blk_step, (zero, zero))
    starts_ref[...] = jnp.where(blk_lane >= n_blk, c_sel.astype(I32), starts_ref[...])


def _select_call(aff_t, cap):
    n_exp, s = aff_t.shape
    t = min(ROUTE_BLOCK, s)
    ii = np.arange(t)
    tri = jnp.asarray((ii[:, None] <= ii[None, :]).astype(np.float32), dtype=BF16)
    return pl.pallas_call(
        functools.partial(_select_kernel, float(cap)),
        grid=(1,),
        in_specs=[_full((n_exp, s)), _full((t, t))],
        out_specs=[_full((n_exp, s)), _full((n_exp, LANES))],
        out_shape=[jax.ShapeDtypeStruct((n_exp, s), I32), jax.ShapeDtypeStruct((n_exp, LANES), I32)],
        compiler_params=_cparams("arbitrary"),
        name="route_select",
    )(aff_t, tri)


def _gather_kernel(cap, per, starts_ref, pos_ref, h_ref, xe_ref, acc_ref):
    e = pl.program_id(0)
    b = pl.program_id(1)
    t = h_ref.shape[0]

    @pl.when(b == 0)
    def _():
        acc_ref[...] = jnp.zeros_like(acc_ref)

    lo = starts_ref[e, b * per]
    hi = starts_ref[e, (b + 1) * per]
    c0 = lo // ROW_CHUNK
    pos = pos_ref[pl.ds(e, 1), :]
    slot = lax.broadcasted_iota(I32, (ROW_CHUNK, t), 0)
    for kk in range(t // ROW_CHUNK + 1):
        c = c0 + kk

        @pl.when((hi > lo) & (c * ROW_CHUNK < hi) & (c < cap // ROW_CHUNK))
        def _():
            base = pl.multiple_of(c * ROW_CHUNK, ROW_CHUNK)
            onehot = (pos - base == slot).astype(BF16)
            acc_ref[pl.ds(base, ROW_CHUNK), :] += jnp.dot(onehot, h_ref[...],
                                                          preferred_element_type=F32)

    @pl.when(b == pl.num_programs(1) - 1)
    def _():
        xe_ref[0] = acc_ref[...].astype(BF16)


def _gather_call(starts, pos_t, h, cap):
    s, d = h.shape
    t = min(GATHER_BLOCK, s)
    return pl.pallas_call(
        functools.partial(_gather_kernel, cap, t // min(ROUTE_BLOCK, s)),
        grid_spec=pltpu.PrefetchScalarGridSpec(
            num_scalar_prefetch=1,
            grid=(N_EXPERTS, s // t),
            in_specs=[pl.BlockSpec((N_EXPERTS, t), lambda e, b, st: (0, b)),
                      pl.BlockSpec((t, d), lambda e, b, st: (b, 0))],
            out_specs=pl.BlockSpec((1, cap, d), lambda e, b, st: (e, 0, 0)),
            scratch_shapes=[pltpu.VMEM((cap, d), F32)],
        ),
        out_shape=jax.ShapeDtypeStruct((N_EXPERTS, cap, d), BF16),
        compiler_params=_cparams("parallel", "arbitrary"),
        name="expert_gather",
    )(starts, pos_t, h)


def _ffn_kernel(xe_ref, wg_ref, wu_ref, wd_ref, ye_ref, acc_ref):
    f = pl.program_id(1)
    x = xe_ref[0]
    a = jnp.dot(x, wg_ref[0, 0].astype(BF16), preferred_element_type=F32)
    u = jnp.dot(x, wu_ref[0, 0].astype(BF16), preferred_element_type=F32)
    hid = (_silu(a) * u).astype(BF16)
    contrib = jnp.dot(hid, wd_ref[0, 0].astype(BF16), preferred_element_type=F32)

    @pl.when(f == 0)
    def _():
        acc_ref[...] = contrib

    @pl.when(f > 0)
    def _():
        acc_ref[...] += contrib

    @pl.when(f == pl.num_programs(1) - 1)
    def _():
        ye_ref[0] = acc_ref[...].astype(BF16)


def _ffn_call(layer, xe, w_gate, w_up, w_down):
    n_exp, cap, d = xe.shape
    ff = w_gate.shape[3]
    fb = min(FF_BLOCK, ff)
    return pl.pallas_call(
        _ffn_kernel,
        grid=(n_exp, ff // fb),
        in_specs=[pl.BlockSpec((1, cap, d), lambda e, f: (e, 0, 0)),
                  pl.BlockSpec((1, 1, d, fb), lambda e, f: (layer, e, 0, f)),
                  pl.BlockSpec((1, 1, d, fb), lambda e, f: (layer, e, 0, f)),
                  pl.BlockSpec((1, 1, fb, d), lambda e, f: (layer, e, f, 0))],
        out_specs=pl.BlockSpec((1, cap, d), lambda e, f: (e, 0, 0)),
        out_shape=jax.ShapeDtypeStruct((n_exp, cap, d), BF16),
        scratch_shapes=[pltpu.VMEM((cap, d), F32)],
        compiler_params=_cparams("parallel", "arbitrary"),
        name="expert_ffn",
    )(xe, w_gate, w_up, w_down)


def _combine_kernel(n_chunks, n_cand, starts_ref, pos_ref, w_ref, x_ref, g2_ref, ye_ref, xo_ref,
                    acc_ref, buf_ref, sem_ref):
    b = pl.program_id(0)
    t = x_ref.shape[0]
    n_exp = pos_ref.shape[1]

    def first_chunk(e):
        return starts_ref[e, b] // ROW_CHUNK

    def needed(e, kk):
        lo = starts_ref[e, b]
        hi = starts_ref[e, b + 1]
        c = first_chunk(e) + kk
        return (hi > lo) & (c * ROW_CHUNK < hi) & (c < n_chunks)

    def chunk_copy(e, kk, slot):
        c = jnp.minimum(first_chunk(e) + kk, n_chunks - 1)
        rows = pl.ds(pl.multiple_of(c * ROW_CHUNK, ROW_CHUNK), ROW_CHUNK)
        return pltpu.make_async_copy(ye_ref.at[e, rows, :], buf_ref.at[slot, kk],
                                     sem_ref.at[slot, kk])

    def start_expert(e, slot):
        for kk in range(n_cand):
            @pl.when(needed(e, kk))
            def _():
                chunk_copy(e, kk, slot).start()

    def wait_expert(e, slot):
        for kk in range(n_cand):
            @pl.when(needed(e, kk))
            def _():
                chunk_copy(e, kk, slot).wait()

    acc_ref[...] = jnp.zeros_like(acc_ref)
    start_expert(0, 0)
    lane = lax.broadcasted_iota(I32, pos_ref.shape, 1)
    row_slot = lax.broadcasted_iota(I32, (t, ROW_CHUNK), 1)

    def expert_step(e, carry):
        slot = lax.rem(e, 2)

        @pl.when(e + 1 < n_exp)
        def _():
            start_expert(e + 1, 1 - slot)

        wait_expert(e, slot)
        pos = jnp.sum(jnp.where(lane == e, pos_ref[...], 0), axis=1, keepdims=True)
        wgt = jnp.sum(jnp.where(lane == e, w_ref[...], 0.0), axis=1, keepdims=True)
        for kk in range(n_cand):
            @pl.when(needed(e, kk))
            def _():
                base = (first_chunk(e) + kk) * ROW_CHUNK
                onehot = (pos - base == row_slot).astype(BF16)
                acc_ref[...] += wgt * jnp.dot(onehot, buf_ref[slot, kk],
                                              preferred_element_type=F32)
        return carry

    lax.fori_loop(0, n_exp, expert_step, 0)
    xo_ref[...] = x_ref[...] + g2_ref[...] * acc_ref[...]


def _combine_call(starts, pos_tok, w_tok, x, gate2, ye):
    s, d = x.shape
    n_exp, cap, _ = ye.shape
    t = min(ROUTE_BLOCK, s)
    n_chunks = cap // ROW_CHUNK
    n_cand = min(t // ROW_CHUNK + 1, n_chunks)
    return pl.pallas_call(
        functools.partial(_combine_kernel, n_chunks, n_cand),
        grid_spec=pltpu.PrefetchScalarGridSpec(
            num_scalar_prefetch=1,
            grid=(s // t,),
            in_specs=[pl.BlockSpec((t, n_exp), lambda b, st: (b, 0)),
                      pl.BlockSpec((t, n_exp), lambda b, st: (b, 0)),
                      pl.BlockSpec((t, d), lambda b, st: (b, 0)),
                      pl.BlockSpec((1, d), lambda b, st: (0, 0)),
                      pl.BlockSpec(memory_space=pl.ANY)],
            out_specs=pl.BlockSpec((t, d), lambda b, st: (b, 0)),
            scratch_shapes=[pltpu.VMEM((t, d), F32),
                            pltpu.VMEM((2, n_cand, ROW_CHUNK, d), BF16),
                            pltpu.SemaphoreType.DMA((2, n_cand))],
        ),
        out_shape=jax.ShapeDtypeStruct((s, d), F32),
        compiler_params=_cparams("arbitrary"),
        name="expert_combine",
    )(starts, pos_tok, w_tok, x, gate2, ye)


def _rope_tables(seq_len):
    rows = seq_len // GRID_W
    row_id = jnp.repeat(jnp.arange(rows, dtype=F32), GRID_W)
    col_id = jnp.tile(jnp.arange(GRID_W, dtype=F32), rows)
    inv_freq = ROPE_THETA ** (-jnp.arange(0, ROPE_AXIS_DIM, 2, dtype=F32) / ROPE_AXIS_DIM)
    ang = jnp.stack([row_id[:, None] * inv_freq, col_id[:, None] * inv_freq], axis=1)
    cos, sin = jnp.cos(ang), jnp.sin(ang)
    cos_h = jnp.concatenate([cos, cos], axis=-1).reshape(seq_len, ATT_HEAD_DIM)
    sin_h = jnp.concatenate([-sin, sin], axis=-1).reshape(seq_len, ATT_HEAD_DIM)
    reps = LANES // ATT_HEAD_DIM
    return jnp.tile(cos_h, (1, reps)), jnp.tile(sin_h, (1, reps))


def _head_mean_operator():
    ii = np.arange(ATT_WIDTH)
    same = (ii[:, None] // ATT_HEAD_DIM) == (ii[None, :] // ATT_HEAD_DIM)
    return jnp.asarray(same.astype(np.float32) / ATT_HEAD_DIM)


def kernel(x, c, ada_w, ada_b, norm_mix_w, norm_ffn_w, w_in, q_norm_w, k_norm_w, hg_lower_bounds,
           hg_norm_w, w_branch_att, w_branch_hg, w_out, w_router, w_exp_gate, w_exp_up, w_exp_down):
    batch, s, d = x.shape
    assert batch == 1, "kernel is written for a single sequence"
    n_layers = ada_w.shape[0]
    cap = EC_CAPACITY_FACTOR * s // N_EXPERTS
    assert cap % ROW_CHUNK == 0 and s % min(ROUTE_BLOCK, s) == 0

    cos, sin = _rope_tables(s)
    bd = _head_mean_operator()
    mods = _ada_mod(c, ada_w, ada_b)
    qkv_w = ATT_WIDTH + 2 * ATT_KV_WIDTH
    hg_w = 5 * HG_WIDTH
    xs = x[0]
    for l in range(n_layers):
        shift1, scale1, gate1, shift2, scale2, gate2 = [
            mods[l, i * d:(i + 1) * d].reshape(1, d) for i in range(6)]
        qk_w = ATT_WIDTH + ATT_KV_WIDTH
        h, q, k, v = _qkv_call(
            xs, shift1, scale1, norm_mix_w[l].reshape(1, d), w_in[l, :, :qk_w].astype(BF16),
            w_in[l, :, qk_w:qkv_w].T.astype(BF16),
            jnp.tile(q_norm_w[l], ATT_HEADS).reshape(1, ATT_WIDTH),
            jnp.tile(k_norm_w[l], ATT_KV_HEADS).reshape(1, ATT_KV_WIDTH), cos, sin, bd)
        hq, kf, gf, kb, gb, hv, og = _hg_proj_call(
            l, h, w_in[l, :, qkv_w:qkv_w + hg_w].astype(BF16), hg_lower_bounds)
        g = _gate_call(h, w_in[l, :, qkv_w + hg_w:].astype(BF16))
        o_att = _attn_call(q, k, v)
        o_hg = _hg_call(hq, kf, gf, kb, gb, hv, og, hg_norm_w[l].reshape(1, HG_DIM))
        xs, h2, aff_t = _merge_call(
            xs, o_att, o_hg, g, w_branch_att[l].astype(BF16), w_branch_hg[l].astype(BF16),
            w_out[l].astype(BF16), gate1, shift2, scale2, norm_ffn_w[l].reshape(1, d),
            w_router[l].T)
        pos_t, starts = _select_call(aff_t, cap)
        xe = _gather_call(starts, pos_t, h2, cap)
        ye = _ffn_call(l, xe, w_exp_gate, w_exp_up, w_exp_down)
        xs = _combine_call(starts, pos_t.T, aff_t.T, xs, gate2, ye)
    return xs[None]
```

```python
import functools

import numpy as np
import jax
import jax.numpy as jnp
from jax import lax
from jax.experimental import pallas as pl
from jax.experimental.pallas import tpu as pltpu

F32 = jnp.float32
BF16 = jnp.bfloat16
I32 = jnp.int32
HIGHEST = lax.Precision.HIGHEST

GRID_W = 64
ATT_HEADS = 8
ATT_KV_HEADS = 2
ATT_GROUP = ATT_HEADS // ATT_KV_HEADS
ATT_HEAD_DIM = 64
ATT_WIDTH = ATT_HEADS * ATT_HEAD_DIM
ATT_KV_WIDTH = ATT_KV_HEADS * ATT_HEAD_DIM
ROPE_AXIS_DIM = ATT_HEAD_DIM // 2
ROPE_THETA = 10000.0
HG_HEADS = 4
HG_DIM = 128
HG_WIDTH = HG_HEADS * HG_DIM
MIN_FORGET = 1e-6
N_EXPERTS = 16
EC_CAPACITY_FACTOR = 2
NORM_EPS = 1e-6
LOG2_E = 1.4426950408889634

LANES = 128
VMEM_LIMIT_BYTES = 56 * 1024 * 1024

TOKEN_BLOCK = 512
ATT_Q_BLOCK = 512
ATT_KV_BLOCK = 512
ATT_ROW_PARTS = 4
ATT_ONES_ROWS = 16
ATT_MAX_LAG = 64.0
HG_CHUNK = 128
HG_SUB = 16
ROUTE_BLOCK = 512
GATHER_BLOCK = 1024
ROW_CHUNK = 256
FF_BLOCK = 512
FF_ROW_PARTS = 2


def _cparams(*sem):
    return pltpu.CompilerParams(dimension_semantics=sem, vmem_limit_bytes=VMEM_LIMIT_BYTES)


def _sigmoid(x):
    return 1.0 / (1.0 + jnp.exp(-x))


def _silu(x):
    return x * _sigmoid(x)


def _full(shape):
    return pl.BlockSpec(shape, lambda *_: (0,) * len(shape))


def _ada_kernel(c_ref, w_ref, b_ref, o_ref):
    ca = _silu(c_ref[...])
    o_ref[0] = jnp.dot(ca, w_ref[0], precision=HIGHEST, preferred_element_type=F32) + b_ref[0]


def _ada_mod(c, ada_w, ada_b):
    n_layers, d, w6 = ada_w.shape
    col = 1536
    c8 = jnp.broadcast_to(c, (8, d))
    out = pl.pallas_call(
        _ada_kernel,
        grid=(n_layers, w6 // col),
        in_specs=[
            pl.BlockSpec((8, d), lambda l, j: (0, 0)),
            pl.BlockSpec((1, d, col), lambda l, j: (l, 0, j)),
            pl.BlockSpec((1, 1, col), lambda l, j: (l, 0, j)),
        ],
        out_specs=pl.BlockSpec((1, 8, col), lambda l, j: (l, 0, j)),
        out_shape=jax.ShapeDtypeStruct((n_layers, 8, w6), F32),
        compiler_params=_cparams("parallel", "parallel"),
        name="ada_mod",
    )(c8, ada_w, ada_b.reshape(n_layers, 1, w6))
    return out[:, 0, :]


def _rms_mod(x, nw, scale, shift):
    ms = jnp.mean(x * x, axis=-1, keepdims=True)
    return (x * lax.rsqrt(ms + NORM_EPS) * nw) * (1.0 + scale) + shift


def _qkv_kernel(x_ref, sh_ref, sc_ref, nw_ref, w_ref, wvt_ref, qnw_ref, knw_ref, cos_ref,
                sin_ref, bd_ref, h_ref, q_ref, k_ref, vt_ref):
    h = _rms_mod(x_ref[...], nw_ref[...], sc_ref[...], sh_ref[...])
    hb = h.astype(BF16)
    h_ref[...] = hb
    proj = jnp.dot(hb, w_ref[...], preferred_element_type=F32)
    aq = proj[:, :ATT_WIDTH]
    ak = proj[:, ATT_WIDTH:]
    avt = lax.dot_general(wvt_ref[...], hb, (((1,), (1,)), ((), ())),
                          preferred_element_type=F32)
    bd = bd_ref[...]
    qms = jnp.dot(aq * aq, bd, precision=HIGHEST, preferred_element_type=F32)
    kms = jnp.dot(ak * ak, bd[:ATT_KV_WIDTH, :ATT_KV_WIDTH], precision=HIGHEST,
                  preferred_element_type=F32)
    qn = aq * lax.rsqrt(qms + NORM_EPS) * qnw_ref[...]
    kn = ak * lax.rsqrt(kms + NORM_EPS) * knw_ref[...]
    cos = cos_ref[...]
    sin = sin_ref[...]
    lane = lax.broadcasted_iota(I32, cos.shape, 1)
    first_half = (lane % ROPE_AXIS_DIM) < (ROPE_AXIS_DIM // 2)

    def rope(xs):
        up = pltpu.roll(xs, LANES - ROPE_AXIS_DIM // 2, 1)
        dn = pltpu.roll(xs, ROPE_AXIS_DIM // 2, 1)
        return xs * cos + jnp.where(first_half, up, dn) * sin

    q_scale = ATT_HEAD_DIM ** -0.5 * LOG2_E
    for j in range(ATT_WIDTH // LANES):
        qr = rope(qn[:, j * LANES:(j + 1) * LANES]) * q_scale
        q_ref[2 * j] = qr[:, :ATT_HEAD_DIM].astype(BF16)
        q_ref[2 * j + 1] = qr[:, ATT_HEAD_DIM:].astype(BF16)
    kr = rope(kn)
    ones = jnp.ones((ATT_ONES_ROWS, avt.shape[1]), F32)
    for j in range(ATT_KV_HEADS):
        k_ref[j] = kr[:, j * ATT_HEAD_DIM:(j + 1) * ATT_HEAD_DIM].astype(BF16)
        vj = avt[j * ATT_HEAD_DIM:(j + 1) * ATT_HEAD_DIM]
        vt_ref[j] = jnp.concatenate([vj, ones], axis=0).astype(BF16)


def _qkv_call(x, shift, scale, nw, w_qk, w_v_t, qnw, knw, cos, sin, bd):
    s, d = x.shape
    t = min(TOKEN_BLOCK, s)
    vr = ATT_HEAD_DIM + ATT_ONES_ROWS
    row = lambda i: (i, 0)
    head = lambda i: (0, i, 0)
    return pl.pallas_call(
        _qkv_kernel,
        grid=(s // t,),
        in_specs=[
            pl.BlockSpec((t, d), row), _full((1, d)), _full((1, d)), _full((1, d)),
            _full(w_qk.shape), _full(w_v_t.shape), _full((1, ATT_WIDTH)), _full((1, ATT_KV_WIDTH)),
            pl.BlockSpec((t, LANES), row), pl.BlockSpec((t, LANES), row), _full(bd.shape),
        ],
        out_specs=[
            pl.BlockSpec((t, d), row),
            pl.BlockSpec((ATT_HEADS, t, ATT_HEAD_DIM), head),
            pl.BlockSpec((ATT_KV_HEADS, t, ATT_HEAD_DIM), head),
            pl.BlockSpec((ATT_KV_HEADS, vr, t), lambda i: (0, 0, i)),
        ],
        out_shape=[
            jax.ShapeDtypeStruct((s, d), BF16),
            jax.ShapeDtypeStruct((ATT_HEADS, s, ATT_HEAD_DIM), BF16),
            jax.ShapeDtypeStruct((ATT_KV_HEADS, s, ATT_HEAD_DIM), BF16),
            jax.ShapeDtypeStruct((ATT_KV_HEADS, vr, s), BF16),
        ],
        compiler_params=_cparams("parallel"),
        name="qkv_proj",
    )(x, shift, scale, nw, w_qk, w_v_t, qnw, knw, cos, sin, bd)


def _hg_proj_kernel(layer, h_ref, w_ref, lb_ref, q_ref, kf_ref, gf_ref, kb_ref, gb_ref,
                    v_ref, og_ref):
    hb = h_ref[...]

    def proj(j):
        return jnp.dot(hb, w_ref[:, j * HG_WIDTH:(j + 1) * HG_WIDTH], preferred_element_type=F32)

    lb_raw = lb_ref[...]
    n_layers = lb_raw.shape[0]
    mx = lb_raw[0]
    for i in range(1, n_layers):
        mx = jnp.maximum(mx, lb_raw[i])
    ex = [jnp.exp(lb_raw[i] - mx) for i in range(n_layers)]
    den = ex[0]
    for i in range(1, n_layers):
        den = den + ex[i]
    lb = jnp.zeros_like(mx)
    for i in range(1, layer + 1):
        lb = lb + ex[i] / den

    def forget(z, lbd):
        f = lbd + (1.0 - lbd) * _sigmoid(z)
        return jnp.log(jnp.maximum(f, MIN_FORGET)), (1.0 - lbd) * _sigmoid(-z)

    q_ref[...] = _silu(proj(0))
    gf, kf = forget(proj(1), lb[0:1])
    gf_ref[...] = gf
    kf_ref[...] = kf
    gb, kb = forget(proj(2), lb[1:2])
    gb_ref[...] = gb
    kb_ref[...] = kb
    v_ref[...] = proj(3)
    og_ref[...] = _silu(proj(4))


def _hg_proj_call(layer, h, w_hg, lb_raw):
    s, d = h.shape
    t = min(TOKEN_BLOCK, s)
    row = lambda i: (i, 0)
    out = jax.ShapeDtypeStruct((s, HG_WIDTH), F32)
    return pl.pallas_call(
        functools.partial(_hg_proj_kernel, layer),
        grid=(s // t,),
        in_specs=[pl.BlockSpec((t, d), row), _full(w_hg.shape), _full(lb_raw.shape)],
        out_specs=[pl.BlockSpec((t, HG_WIDTH), row)] * 7,
        out_shape=[out] * 7,
        compiler_params=_cparams("parallel"),
        name="hg_proj",
    )(h, w_hg, lb_raw)


def _gate_kernel(h_ref, w_ref, g_ref):
    g_ref[...] = _sigmoid(jnp.dot(h_ref[...], w_ref[...], preferred_element_type=F32)).astype(BF16)


def _gate_call(h, w_gate):
    s, d = h.shape
    t = min(TOKEN_BLOCK, s)
    n = w_gate.shape[1]
    return pl.pallas_call(
        _gate_kernel,
        grid=(s // t,),
        in_specs=[pl.BlockSpec((t, d), lambda i: (i, 0)), _full(w_gate.shape)],
        out_specs=pl.BlockSpec((t, n), lambda i: (i, 0)),
        out_shape=jax.ShapeDtypeStruct((s, n), BF16),
        compiler_params=_cparams("parallel"),
        name="gate_proj",
    )(h, w_gate)


def _attn_kernel(q_ref, k_ref, vt_ref, o_ref, m_ref, acc_ref):
    g, tq, hd = q_ref.shape
    s = k_ref.shape[1]
    tk = min(ATT_KV_BLOCK, s)
    heads_per_part = g // ATT_ROW_PARTS
    cols = heads_per_part * tq
    m_ref[...] = jnp.full(m_ref.shape, -jnp.inf, F32)
    acc_ref[...] = jnp.zeros(acc_ref.shape, F32)

    def scores(ks, r):
        q = q_ref[r * heads_per_part:(r + 1) * heads_per_part].reshape(cols, hd)
        return lax.dot_general(ks, q, (((1,), (1,)), ((), ())), preferred_element_type=F32)

    def body(j, carry):
        start = pl.multiple_of(j * tk, tk)
        ks = k_ref[0, pl.ds(start, tk), :]
        vt = vt_ref[0, :, pl.ds(start, tk)]
        parts = [pl.ds(r * cols, cols) for r in range(ATT_ROW_PARTS)]
        sts = [scores(ks, r) for r in range(ATT_ROW_PARTS)]
        pvs, m_news, lag = [], [], []
        for sl, st in zip(parts, sts):
            m_old = m_ref[:, sl]
            m_blk = jnp.max(st, axis=0, keepdims=True)
            p = jnp.exp2(st - m_old).astype(BF16)
            pvs.append(jnp.dot(vt, p, preferred_element_type=F32))
            m_news.append(jnp.maximum(m_old, m_blk))
            lag.append(jnp.max(m_blk - m_old))
        worst_lag = functools.reduce(jnp.maximum, lag)
        safe = worst_lag <= ATT_MAX_LAG

        @pl.when(safe)
        def _():
            for sl, pv, m_new in zip(parts, pvs, m_news):
                acc_ref[:, sl] = (acc_ref[:, sl] + pv) * jnp.exp2(m_ref[:, sl] - m_new)
                m_ref[:, sl] = m_new

        @pl.when(jnp.logical_not(safe))
        def _():
            for r, sl in enumerate(parts):
                st = scores(ks, r)
                m_old = m_ref[:, sl]
                m_new = jnp.maximum(m_old, jnp.max(st, axis=0, keepdims=True))
                p = jnp.exp2(st - m_new).astype(BF16)
                acc_ref[:, sl] = (jnp.exp2(m_old - m_new) * acc_ref[:, sl]
                                  + jnp.dot(vt, p, preferred_element_type=F32))
                m_ref[:, sl] = m_new
        return carry

    lax.fori_loop(0, s // tk, body, 0)
    acc = acc_ref[...]
    o_t = acc[:hd] / acc[hd:hd + 1]
    for i in range(g):
        o_ref[:, i * hd:(i + 1) * hd] = o_t[:, i * tq:(i + 1) * tq].T.astype(BF16)


def _attn_call(q, k, vt):
    _, s, hd = q.shape
    vr = vt.shape[1]
    tq = min(ATT_Q_BLOCK, s)
    return pl.pallas_call(
        _attn_kernel,
        grid=(ATT_KV_HEADS, s // tq),
        in_specs=[
            pl.BlockSpec((ATT_GROUP, tq, hd), lambda h, i: (h, i, 0)),
            pl.BlockSpec((1, s, hd), lambda h, i: (h, 0, 0)),
            pl.BlockSpec((1, vr, s), lambda h, i: (h, 0, 0)),
        ],
        out_specs=pl.BlockSpec((tq, ATT_GROUP * hd), lambda h, i: (i, h)),
        out_shape=jax.ShapeDtypeStruct((s, ATT_HEADS * hd), BF16),
        scratch_shapes=[pltpu.VMEM((1, ATT_GROUP * tq), F32),
                        pltpu.VMEM((vr, ATT_GROUP * tq), F32)],
        compiler_params=_cparams("parallel", "parallel"),
        name="attention",
    )(q, k, vt)


def _chunk_cumsum(g, reverse):
    c = g.shape[0]
    row = lax.broadcasted_iota(I32, g.shape, 0)
    x = g
    step = 1
    while step < c:
        if not reverse:
            x = x + jnp.where(row >= step, pltpu.roll(x, step, 0), 0.0)
        else:
            x = x + jnp.where(row < c - step, pltpu.roll(x, c - step, 0), 0.0)
        step *= 2
    return x


def _boundary_rows(b_ref, half, idx):
    c, dd = b_ref.shape
    blk = 2 * half
    pieces = []
    if blk >= 8:
        for s0 in range(0, c, blk):
            pieces.append(jnp.broadcast_to(b_ref[pl.ds(s0 + idx, 1), :], (blk, dd)))
    else:
        sub = lax.broadcasted_iota(I32, (8, dd), 0)
        for v0 in range(0, c, 8):
            piece = jnp.broadcast_to(b_ref[pl.ds(v0 + idx, 1), :], (8, dd))
            for s0 in range(v0 + blk, v0 + 8, blk):
                row = jnp.broadcast_to(b_ref[pl.ds(s0 + idx, 1), :], (8, dd))
                piece = jnp.where(sub >= s0 - v0, row, piece)
            pieces.append(piece)
    return jnp.concatenate(pieces, axis=0)


def _hg_scan_body(reverse, q_ref, k_ref, v_ref, g_ref, mask_ref, st_ref, b_ref):
    c = q_ref.shape[0]
    heads = [slice(h * HG_DIM, (h + 1) * HG_DIM) for h in range(q_ref.shape[1] // HG_DIM)]
    nt = (((1,), (1,)), ((), ()))
    q = q_ref[...]
    k = k_ref[...]
    vb = v_ref[...].astype(BF16)
    b = _chunk_cumsum(g_ref[...] * LOG2_E, reverse)
    b_ref[...] = b
    b_edge = b_ref[pl.ds(c - 1 if not reverse else 0, 1), :]
    qh = (q * jnp.exp2(b)).astype(BF16)
    kh = (k * jnp.exp2(b_edge - b)).astype(BF16)
    st_decay = jnp.exp2(b_edge)
    outs = []
    for h, sl in enumerate(heads):
        st = st_ref[h]
        outs.append(lax.dot_general(qh[:, sl], st.astype(BF16), nt, preferred_element_type=F32))
        st_ref[h] = st * st_decay[:, sl] + lax.dot_general(
            vb[:, sl], kh[:, sl], (((0,), (0,)), ((), ())), preferred_element_type=F32)

    qb = q.astype(BF16)
    kb = k.astype(BF16)
    scores = [mask_ref[0] * lax.dot_general(qb[:, sl], kb[:, sl], nt, preferred_element_type=F32)
              for sl in heads]
    half = c // 2
    level = 1
    while half >= 1:
        d = b - _boundary_rows(b_ref, half, half - 1 if not reverse else half)
        qt = (q * jnp.exp2(jnp.minimum(d, 0.0))).astype(BF16)
        kt = (k * jnp.exp2(jnp.minimum(-d, 0.0))).astype(BF16)
        mask = mask_ref[level]
        scores = [a + mask * lax.dot_general(qt[:, sl], kt[:, sl], nt, preferred_element_type=F32)
                  for a, sl in zip(scores, heads)]
        half //= 2
        level += 1
    return [o + jnp.dot(a.astype(BF16), vb[:, sl], preferred_element_type=F32)
            for o, a, sl in zip(outs, scores, heads)]


def _hg_fwd_kernel(q_ref, k_ref, v_ref, g_ref, mask_ref, o_ref, st_ref, b_ref):
    @pl.when(pl.program_id(0) == 0)
    def _():
        st_ref[...] = jnp.zeros_like(st_ref)

    outs = _hg_scan_body(False, q_ref, k_ref, v_ref, g_ref, mask_ref, st_ref, b_ref)
    o_ref[...] = jnp.concatenate(outs, axis=1)


def _hg_bwd_kernel(q_ref, k_ref, v_ref, g_ref, mask_ref, of_ref, og_ref, nw_ref, o_ref, st_ref,
                   b_ref):
    @pl.when(pl.program_id(0) == 0)
    def _():
        st_ref[...] = jnp.zeros_like(st_ref)

    outs = _hg_scan_body(True, q_ref, k_ref, v_ref, g_ref, mask_ref, st_ref, b_ref)
    normed = []
    for h, o_b in enumerate(outs):
        o = of_ref[:, h * HG_DIM:(h + 1) * HG_DIM] + o_b
        ms = jnp.mean(o * o, axis=-1, keepdims=True)
        normed.append(o * lax.rsqrt(ms + NORM_EPS) * nw_ref[...])
    o_ref[...] = (jnp.concatenate(normed, axis=1) * og_ref[...]).astype(BF16)


def _level_masks(c, reverse):
    ii = np.arange(c)[:, None]
    jj = np.arange(c)[None, :]
    masks = [ii == jj]
    half = c // 2
    while half >= 1:
        blk = 2 * half
        same = (ii // blk) == (jj // blk)
        q_hi, k_hi = (ii % blk) >= half, (jj % blk) >= half
        masks.append(same & (q_hi & ~k_hi if not reverse else ~q_hi & k_hi))
        half //= 2
    return jnp.asarray(np.stack(masks).astype(np.float32))


def _hg_call(q, kf, gf, kb, gb, v, og, nw):
    s = q.shape[0]
    c = min(HG_CHUNK, s)
    n = s // c
    mask_f = _level_masks(c, False)
    mask_b = _level_masks(c, True)
    fwd = lambda i: (i, 0)
    bwd = lambda i: (n - 1 - i, 0)
    blk = lambda m: pl.BlockSpec((c, HG_WIDTH), m)
    scratch = [pltpu.VMEM((HG_HEADS, HG_DIM, HG_DIM), F32), pltpu.VMEM((c, HG_WIDTH), F32)]
    o_fwd = pl.pallas_call(
        _hg_fwd_kernel,
        grid=(n,),
        in_specs=[blk(fwd)] * 4 + [_full(mask_f.shape)],
        out_specs=blk(fwd),
        out_shape=jax.ShapeDtypeStruct((s, HG_WIDTH), F32),
        scratch_shapes=scratch,
        compiler_params=_cparams("arbitrary"),
        name="hg_scan_fwd",
    )(q, kf, v, gf, mask_f)
    return pl.pallas_call(
        _hg_bwd_kernel,
        grid=(n,),
        in_specs=[blk(bwd)] * 4 + [_full(mask_b.shape), blk(bwd), blk(bwd), _full((1, HG_DIM))],
        out_specs=blk(bwd),
        out_shape=jax.ShapeDtypeStruct((s, HG_WIDTH), BF16),
        scratch_shapes=scratch,
        compiler_params=_cparams("arbitrary"),
        name="hg_scan_bwd",
    )(q, kb, v, gb, mask_b, o_fwd, og, nw)


def _merge_kernel(x_ref, oa_ref, oh_ref, g_ref, wa_ref, wh_ref, wo_ref, g1_ref, sh_ref, sc_ref,
                  nw_ref, wr_ref, xo_ref, h_ref, aff_ref):
    d = x_ref.shape[1]
    ya = jnp.dot(oa_ref[...], wa_ref[...], preferred_element_type=F32)
    yh = jnp.dot(oh_ref[...], wh_ref[...], preferred_element_type=F32)
    merged = g_ref[:, :d].astype(F32) * ya + g_ref[:, d:].astype(F32) * yh
    y = jnp.dot(merged.astype(BF16), wo_ref[...], preferred_element_type=F32)
    xn = x_ref[...] + g1_ref[...] * y
    xo_ref[...] = xn
    h = _rms_mod(xn, nw_ref[...], sc_ref[...], sh_ref[...])
    h_ref[...] = h.astype(BF16)
    logits = lax.dot_general(wr_ref[...], h, (((1,), (1,)), ((), ())), precision=HIGHEST,
                             preferred_element_type=F32)
    ex = jnp.exp(logits - jnp.max(logits, axis=0, keepdims=True))
    aff_ref[...] = ex / jnp.sum(ex, axis=0, keepdims=True)


def _merge_call(x, o_att, o_hg, g, w_ba, w_bh, w_out, gate1, shift2, scale2, nw2, w_router_t):
    s, d = x.shape
    t = min(TOKEN_BLOCK, s)
    row = lambda i: (i, 0)
    return pl.pallas_call(
        _merge_kernel,
        grid=(s // t,),
        in_specs=[
            pl.BlockSpec((t, d), row), pl.BlockSpec((t, ATT_WIDTH), row),
            pl.BlockSpec((t, HG_WIDTH), row), pl.BlockSpec((t, 2 * d), row),
            _full(w_ba.shape), _full(w_bh.shape), _full(w_out.shape),
            _full((1, d)), _full((1, d)), _full((1, d)), _full((1, d)), _full(w_router_t.shape),
        ],
        out_specs=[pl.BlockSpec((t, d), row), pl.BlockSpec((t, d), row),
                   pl.BlockSpec((N_EXPERTS, t), lambda i: (0, i))],
        out_shape=[jax.ShapeDtypeStruct((s, d), F32), jax.ShapeDtypeStruct((s, d), BF16),
                   jax.ShapeDtypeStruct((N_EXPERTS, s), F32)],
        compiler_params=_cparams("parallel"),
        name="merge_router",
    )(x, o_att, o_hg, g, w_ba, w_bh, w_out, gate1, shift2, scale2, nw2, w_router_t)


def _select_kernel(cap, aff_ref, tri_ref, pos_ref, starts_ref):
    n_exp, s = aff_ref.shape
    t = tri_ref.shape[0]
    n_blk = s // t
    aff = aff_ref[...]

    def bit_step(i, thr_bits):
        cand = thr_bits | lax.shift_left(jnp.int32(1), jnp.int32(30) - i)
        cnt = jnp.sum((aff >= pltpu.bitcast(cand, F32)).astype(F32), axis=1, keepdims=True)
        return jnp.where(cnt >= cap, cand, thr_bits)

    thr = pltpu.bitcast(lax.fori_loop(0, 31, bit_step, jnp.zeros((n_exp, 1), I32)), F32)
    n_gt = jnp.sum((aff > thr).astype(F32), axis=1, keepdims=True)
    need_eq = cap - n_gt
    tri = tri_ref[...]
    blk_lane = lax.broadcasted_iota(I32, starts_ref.shape, 1)

    starts_ref[...] = jnp.zeros(starts_ref.shape, I32)

    def blk_step(j, carry):
        c_eq, c_sel = carry
        start = pl.multiple_of(j * t, t)
        bt = aff_ref[:, pl.ds(start, t)]
        eq = bt == thr
        eq_f = eq.astype(BF16)
        eq_excl = c_eq + jnp.dot(eq_f, tri, preferred_element_type=F32) - eq_f.astype(F32)
        sel = (bt > thr) | (eq & (eq_excl < need_eq))
        sel_f = sel.astype(BF16)
        sel_incl = c_sel + jnp.dot(sel_f, tri, preferred_element_type=F32)
        pos_ref[:, pl.ds(start, t)] = jnp.where(sel, sel_incl - 1.0, -1.0).astype(I32)
        starts_ref[...] = jnp.where(blk_lane == j, c_sel.astype(I32), starts_ref[...])
        return (c_eq + jnp.sum(eq_f.astype(F32), axis=1, keepdims=True),
                c_sel + jnp.sum(sel_f.astype(F32), axis=1, keepdims=True))

    zero = jnp.zeros((n_exp, 1), F32)
    _, c_sel = lax.fori_loop(0, n_blk, blk_step, (zero, zero))
    starts_ref[...] = jnp.where(blk_lane >= n_blk, c_sel.astype(I32), starts_ref[...])


def _select_call(aff_t, cap):
    n_exp, s = aff_t.shape
    t = min(ROUTE_BLOCK, s)
    ii = np.arange(t)
    tri = jnp.asarray((ii[:, None] <= ii[None, :]).astype(np.float32), dtype=BF16)
    return pl.pallas_call(
        functools.partial(_select_kernel, float(cap)),
        grid=(1,),
        in_specs=[_full((n_exp, s)), _full((t, t))],
        out_specs=[_full((n_exp, s)), _full((n_exp, LANES))],
        out_shape=[jax.ShapeDtypeStruct((n_exp, s), I32), jax.ShapeDtypeStruct((n_exp, LANES), I32)],
        compiler_params=_cparams("arbitrary"),
        name="route_select",
    )(aff_t, tri)


def _gather_kernel(cap, per, starts_ref, pos_ref, h_ref, xe_ref, acc_ref):
    e = pl.program_id(0)
    b = pl.program_id(1)
    t = h_ref.shape[0]

    @pl.when(b == 0)
    def _():
        acc_ref[...] = jnp.zeros_like(acc_ref)

    lo = starts_ref[e, b * per]
    hi = starts_ref[e, (b + 1) * per]
    c0 = lo // ROW_CHUNK
    pos = pos_ref[pl.ds(e, 1), :]
    slot = lax.broadcasted_iota(I32, (ROW_CHUNK, t), 0)
    for kk in range(t // ROW_CHUNK + 1):
        c = c0 + kk

        @pl.when((hi > lo) & (c * ROW_CHUNK < hi) & (c < cap // ROW_CHUNK))
        def _():
            base = pl.multiple_of(c * ROW_CHUNK, ROW_CHUNK)
            onehot = (pos - base == slot).astype(BF16)
            acc_ref[pl.ds(base, ROW_CHUNK), :] += jnp.dot(onehot, h_ref[...],
                                                          preferred_element_type=F32)

    @pl.when(b == pl.num_programs(1) - 1)
    def _():
        xe_ref[0] = acc_ref[...].astype(BF16)


def _gather_call(starts, pos_t, h, cap):
    s, d = h.shape
    t = min(GATHER_BLOCK, s)
    return pl.pallas_call(
        functools.partial(_gather_kernel, cap, t // min(ROUTE_BLOCK, s)),
        grid_spec=pltpu.PrefetchScalarGridSpec(
            num_scalar_prefetch=1,
            grid=(N_EXPERTS, s // t),
            in_specs=[pl.BlockSpec((N_EXPERTS, t), lambda e, b, st: (0, b)),
                      pl.BlockSpec((t, d), lambda e, b, st: (b, 0))],
            out_specs=pl.BlockSpec((1, cap, d), lambda e, b, st: (e, 0, 0)),
            scratch_shapes=[pltpu.VMEM((cap, d), F32)],
        ),
        out_shape=jax.ShapeDtypeStruct((N_EXPERTS, cap, d), BF16),
        compiler_params=_cparams("parallel", "arbitrary"),
        name="expert_gather",
    )(starts, pos_t, h)


def _ffn_kernel(xe_ref, wg_ref, wu_ref, wd_ref, ye_ref, acc_ref):
    f = pl.program_id(1)
    last = pl.num_programs(1) - 1
    rows = xe_ref.shape[1] // FF_ROW_PARTS
    wg = wg_ref[0, 0].astype(BF16)
    wu = wu_ref[0, 0].astype(BF16)
    wd = wd_ref[0, 0].astype(BF16)
    parts = [pl.ds(r * rows, rows) for r in range(FF_ROW_PARTS)]
    gate_up = []
    for sl in parts:
        x = xe_ref[0, sl, :]
        gate_up.append((jnp.dot(x, wg, preferred_element_type=F32),
                        jnp.dot(x, wu, preferred_element_type=F32)))
    hids = [(_silu(a) * u).astype(BF16) for a, u in gate_up]

    def down(hid):
        return jnp.dot(hid, wd, preferred_element_type=F32)

    @pl.when(f == 0)
    def _():
        for sl, hid in zip(parts, hids):
            acc_ref[sl, :] = down(hid)

    @pl.when((f > 0) & (f < last))
    def _():
        for sl, hid in zip(parts, hids):
            acc_ref[sl, :] += down(hid)

    @pl.when(f == last)
    def _():
        for sl, hid in zip(parts, hids):
            ye_ref[0, sl, :] = (acc_ref[sl, :] + down(hid)).astype(BF16)


def _ffn_call(layer, xe, w_gate, w_up, w_down):
    n_exp, cap, d = xe.shape
    ff = w_gate.shape[3]
    fb = min(FF_BLOCK, ff)
    assert ff // fb >= 2, "the first and last hidden blocks take different accumulate paths"
    return pl.pallas_call(
        _ffn_kernel,
        grid=(n_exp, ff // fb),
        in_specs=[pl.BlockSpec((1, cap, d), lambda e, f: (e, 0, 0)),
                  pl.BlockSpec((1, 1, d, fb), lambda e, f: (layer, e, 0, f)),
                  pl.BlockSpec((1, 1, d, fb), lambda e, f: (layer, e, 0, f)),
                  pl.BlockSpec((1, 1, fb, d), lambda e, f: (layer, e, f, 0))],
        out_specs=pl.BlockSpec((1, cap, d), lambda e, f: (e, 0, 0)),
        out_shape=jax.ShapeDtypeStruct((n_exp, cap, d), BF16),
        scratch_shapes=[pltpu.VMEM((cap, d), F32)],
        compiler_params=_cparams("parallel", "arbitrary"),
        name="expert_ffn",
    )(xe, w_gate, w_up, w_down)


def _combine_kernel(n_chunks, n_cand, starts_ref, pos_ref, w_ref, x_ref, g2_ref, ye_ref, xo_ref,
                    acc_ref, buf_ref, sem_ref):
    b = pl.program_id(0)
    t = x_ref.shape[0]
    n_exp = pos_ref.shape[1]

    def first_chunk(e):
        return starts_ref[e, b] // ROW_CHUNK

    def needed(e, kk):
        lo = starts_ref[e, b]
        hi = starts_ref[e, b + 1]
        c = first_chunk(e) + kk
        return (hi > lo) & (c * ROW_CHUNK < hi) & (c < n_chunks)

    def chunk_copy(e, kk, slot):
        c = jnp.minimum(first_chunk(e) + kk, n_chunks - 1)
        rows = pl.ds(pl.multiple_of(c * ROW_CHUNK, ROW_CHUNK), ROW_CHUNK)
        return pltpu.make_async_copy(ye_ref.at[e, rows, :], buf_ref.at[slot, kk],
                                     sem_ref.at[slot, kk])

    def start_expert(e, slot):
        for kk in range(n_cand):
            @pl.when(needed(e, kk))
            def _():
                chunk_copy(e, kk, slot).start()

    def wait_expert(e, slot):
        for kk in range(n_cand):
            @pl.when(needed(e, kk))
            def _():
                chunk_copy(e, kk, slot).wait()

    acc_ref[...] = jnp.zeros_like(acc_ref)
    start_expert(0, 0)
    lane = lax.broadcasted_iota(I32, pos_ref.shape, 1)
    row_slot = lax.broadcasted_iota(I32, (t, ROW_CHUNK), 1)

    def expert_step(e, carry):
        slot = lax.rem(e, 2)

        @pl.when(e + 1 < n_exp)
        def _():
            start_expert(e + 1, 1 - slot)

        wait_expert(e, slot)
        pos = jnp.sum(jnp.where(lane == e, pos_ref[...], 0), axis=1, keepdims=True)
        wgt = jnp.sum(jnp.where(lane == e, w_ref[...], 0.0), axis=1, keepdims=True)
        for kk in range(n_cand):
            @pl.when(needed(e, kk))
            def _():
                base = (first_chunk(e) + kk) * ROW_CHUNK
                onehot = (pos - base == row_slot).astype(BF16)
                acc_ref[...] += wgt * jnp.dot(onehot, buf_ref[slot, kk],
                                              preferred_element_type=F32)
        return carry

    lax.fori_loop(0, n_exp, expert_step, 0)
    xo_ref[...] = x_ref[...] + g2_ref[...] * acc_ref[...]


def _combine_call(starts, pos_tok, w_tok, x, gate2, ye):
    s, d = x.shape
    n_exp, cap, _ = ye.shape
    t = min(ROUTE_BLOCK, s)
    n_chunks = cap // ROW_CHUNK
    n_cand = min(t // ROW_CHUNK + 1, n_chunks)
    return pl.pallas_call(
        functools.partial(_combine_kernel, n_chunks, n_cand),
        grid_spec=pltpu.PrefetchScalarGridSpec(
            num_scalar_prefetch=1,
            grid=(s // t,),
            in_specs=[pl.BlockSpec((t, n_exp), lambda b, st: (b, 0)),
                      pl.BlockSpec((t, n_exp), lambda b, st: (b, 0)),
                      pl.BlockSpec((t, d), lambda b, st: (b, 0)),
                      pl.BlockSpec((1, d), lambda b, st: (0, 0)),
                      pl.BlockSpec(memory_space=pl.ANY)],
            out_specs=pl.BlockSpec((t, d), lambda b, st: (b, 0)),
            scratch_shapes=[pltpu.VMEM((t, d), F32),
                            pltpu.VMEM((2, n_cand, ROW_CHUNK, d), BF16),
                            pltpu.SemaphoreType.DMA((2, n_cand))],
        ),
        out_shape=jax.ShapeDtypeStruct((s, d), F32),
        compiler_params=_cparams("arbitrary"),
        name="expert_combine",
    )(starts, pos_tok, w_tok, x, gate2, ye)


def _rope_tables(seq_len):
    rows = seq_len // GRID_W
    row_id = jnp.repeat(jnp.arange(rows, dtype=F32), GRID_W)
    col_id = jnp.tile(jnp.arange(GRID_W, dtype=F32), rows)
    inv_freq = ROPE_THETA ** (-jnp.arange(0, ROPE_AXIS_DIM, 2, dtype=F32) / ROPE_AXIS_DIM)
    ang = jnp.stack([row_id[:, None] * inv_freq, col_id[:, None] * inv_freq], axis=1)
    cos, sin = jnp.cos(ang), jnp.sin(ang)
    cos_h = jnp.concatenate([cos, cos], axis=-1).reshape(seq_len, ATT_HEAD_DIM)
    sin_h = jnp.concatenate([-sin, sin], axis=-1).reshape(seq_len, ATT_HEAD_DIM)
    reps = LANES // ATT_HEAD_DIM
    return jnp.tile(cos_h, (1, reps)), jnp.tile(sin_h, (1, reps))


def _head_mean_operator():
    ii = np.arange(ATT_WIDTH)
    same = (ii[:, None] // ATT_HEAD_DIM) == (ii[None, :] // ATT_HEAD_DIM)
    return jnp.asarray(same.astype(np.float32) / ATT_HEAD_DIM)


def kernel(x, c, ada_w, ada_b, norm_mix_w, norm_ffn_w, w_in, q_norm_w, k_norm_w, hg_lower_bounds,
           hg_norm_w, w_branch_att, w_branch_hg, w_out, w_router, w_exp_gate, w_exp_up, w_exp_down):
    batch, s, d = x.shape
    assert batch == 1, "kernel is written for a single sequence"
    n_layers = ada_w.shape[0]
    cap = EC_CAPACITY_FACTOR * s // N_EXPERTS
    assert cap % ROW_CHUNK == 0 and s % min(ROUTE_BLOCK, s) == 0

    cos, sin = _rope_tables(s)
    bd = _head_mean_operator()
    mods = _ada_mod(c, ada_w, ada_b)
    qkv_w = ATT_WIDTH + 2 * ATT_KV_WIDTH
    hg_w = 5 * HG_WIDTH
    xs = x[0]
    for l in range(n_layers):
        shift1, scale1, gate1, shift2, scale2, gate2 = [
            mods[l, i * d:(i + 1) * d].reshape(1, d) for i in range(6)]
        qk_w = ATT_WIDTH + ATT_KV_WIDTH
        h, q, k, v = _qkv_call(
            xs, shift1, scale1, norm_mix_w[l].reshape(1, d), w_in[l, :, :qk_w].astype(BF16),
            w_in[l, :, qk_w:qkv_w].T.astype(BF16),
            jnp.tile(q_norm_w[l], ATT_HEADS).reshape(1, ATT_WIDTH),
            jnp.tile(k_norm_w[l], ATT_KV_HEADS).reshape(1, ATT_KV_WIDTH), cos, sin, bd)
        hq, kf, gf, kb, gb, hv, og = _hg_proj_call(
            l, h, w_in[l, :, qkv_w:qkv_w + hg_w].astype(BF16), hg_lower_bounds)
        g = _gate_call(h, w_in[l, :, qkv_w + hg_w:].astype(BF16))
        o_att = _attn_call(q, k, v)
        o_hg = _hg_call(hq, kf, gf, kb, gb, hv, og, hg_norm_w[l].reshape(1, HG_DIM))
        xs, h2, aff_t = _merge_call(
            xs, o_att, o_hg, g, w_branch_att[l].astype(BF16), w_branch_hg[l].astype(BF16),
            w_out[l].astype(BF16), gate1, shift2, scale2, norm_ffn_w[l].reshape(1, d),
            w_router[l].T)
        pos_t, starts = _select_call(aff_t, cap)
        xe = _gather_call(starts, pos_t, h2, cap)
        ye = _ffn_call(l, xe, w_exp_gate, w_exp_up, w_exp_down)
        xs = _combine_call(starts, pos_t.T, aff_t.T, xs, gate2, ye)
    return xs[None]
```

```python
import functools

import numpy as np
import jax
import jax.numpy as jnp
from jax import lax
from jax.experimental import pallas as pl
from jax.experimental.pallas import tpu as pltpu

F32 = jnp.float32
BF16 = jnp.bfloat16
I32 = jnp.int32
HIGHEST = lax.Precision.HIGHEST

GRID_W = 64
ATT_HEADS = 8
ATT_KV_HEADS = 2
ATT_GROUP = ATT_HEADS // ATT_KV_HEADS
ATT_HEAD_DIM = 64
ATT_WIDTH = ATT_HEADS * ATT_HEAD_DIM
ATT_KV_WIDTH = ATT_KV_HEADS * ATT_HEAD_DIM
ROPE_AXIS_DIM = ATT_HEAD_DIM // 2
ROPE_THETA = 10000.0
HG_HEADS = 4
HG_DIM = 128
HG_WIDTH = HG_HEADS * HG_DIM
MIN_FORGET = 1e-6
N_EXPERTS = 16
EC_CAPACITY_FACTOR = 2
NORM_EPS = 1e-6
LOG2_E = 1.4426950408889634

LANES = 128
VMEM_LIMIT_BYTES = 56 * 1024 * 1024

TOKEN_BLOCK = 512
ATT_Q_BLOCK = 512
ATT_KV_BLOCK = 2048
ATT_ROW_PARTS = 4
ATT_ONES_ROWS = 16
ATT_MAX_LAG = 64.0
ATT_INIT_KEYS = 256
HG_CHUNK = 128
HG_SUB = 16
ROUTE_BLOCK = 512
GATHER_BLOCK = 1024
ROW_CHUNK = 256
FF_BLOCK = 512
FF_ROW_PARTS = 2


def _cparams(*sem):
    return pltpu.CompilerParams(dimension_semantics=sem, vmem_limit_bytes=VMEM_LIMIT_BYTES)


def _sigmoid(x):
    return 1.0 / (1.0 + jnp.exp(-x))


def _silu(x):
    return x * _sigmoid(x)


def _full(shape):
    return pl.BlockSpec(shape, lambda *_: (0,) * len(shape))


def _ada_kernel(c_ref, w_ref, b_ref, o_ref):
    ca = _silu(c_ref[...])
    o_ref[0] = jnp.dot(ca, w_ref[0], precision=HIGHEST, preferred_element_type=F32) + b_ref[0]


def _ada_mod(c, ada_w, ada_b):
    n_layers, d, w6 = ada_w.shape
    col = 1536
    c8 = jnp.broadcast_to(c, (8, d))
    out = pl.pallas_call(
        _ada_kernel,
        grid=(n_layers, w6 // col),
        in_specs=[
            pl.BlockSpec((8, d), lambda l, j: (0, 0)),
            pl.BlockSpec((1, d, col), lambda l, j: (l, 0, j)),
            pl.BlockSpec((1, 1, col), lambda l, j: (l, 0, j)),
        ],
        out_specs=pl.BlockSpec((1, 8, col), lambda l, j: (l, 0, j)),
        out_shape=jax.ShapeDtypeStruct((n_layers, 8, w6), F32),
        compiler_params=_cparams("parallel", "parallel"),
        name="ada_mod",
    )(c8, ada_w, ada_b.reshape(n_layers, 1, w6))
    return out[:, 0, :]


def _rms_mod(x, nw, scale, shift):
    ms = jnp.mean(x * x, axis=-1, keepdims=True)
    return (x * lax.rsqrt(ms + NORM_EPS) * nw) * (1.0 + scale) + shift


def _qkv_kernel(x_ref, sh_ref, sc_ref, nw_ref, w_ref, wvt_ref, qnw_ref, knw_ref, cos_ref,
                sin_ref, bd_ref, h_ref, q_ref, k_ref, vt_ref):
    h = _rms_mod(x_ref[...], nw_ref[...], sc_ref[...], sh_ref[...])
    hb = h.astype(BF16)
    h_ref[...] = hb
    proj = jnp.dot(hb, w_ref[...], preferred_element_type=F32)
    aq = proj[:, :ATT_WIDTH]
    ak = proj[:, ATT_WIDTH:]
    avt = lax.dot_general(wvt_ref[...], hb, (((1,), (1,)), ((), ())),
                          preferred_element_type=F32)
    bd = bd_ref[...]
    qms = jnp.dot(aq * aq, bd, precision=HIGHEST, preferred_element_type=F32)
    kms = jnp.dot(ak * ak, bd[:ATT_KV_WIDTH, :ATT_KV_WIDTH], precision=HIGHEST,
                  preferred_element_type=F32)
    qn = aq * lax.rsqrt(qms + NORM_EPS) * qnw_ref[...]
    kn = ak * lax.rsqrt(kms + NORM_EPS) * knw_ref[...]
    cos = cos_ref[...]
    sin = sin_ref[...]
    lane = lax.broadcasted_iota(I32, cos.shape, 1)
    first_half = (lane % ROPE_AXIS_DIM) < (ROPE_AXIS_DIM // 2)

    def rope(xs):
        up = pltpu.roll(xs, LANES - ROPE_AXIS_DIM // 2, 1)
        dn = pltpu.roll(xs, ROPE_AXIS_DIM // 2, 1)
        return xs * cos + jnp.where(first_half, up, dn) * sin

    q_scale = ATT_HEAD_DIM ** -0.5 * LOG2_E
    for j in range(ATT_WIDTH // LANES):
        qr = rope(qn[:, j * LANES:(j + 1) * LANES]) * q_scale
        q_ref[2 * j] = qr[:, :ATT_HEAD_DIM].astype(BF16)
        q_ref[2 * j + 1] = qr[:, ATT_HEAD_DIM:].astype(BF16)
    kr = rope(kn)
    ones = jnp.ones((ATT_ONES_ROWS, avt.shape[1]), F32)
    for j in range(ATT_KV_HEADS):
        k_ref[j] = kr[:, j * ATT_HEAD_DIM:(j + 1) * ATT_HEAD_DIM].astype(BF16)
        vj = avt[j * ATT_HEAD_DIM:(j + 1) * ATT_HEAD_DIM]
        vt_ref[j] = jnp.concatenate([vj, ones], axis=0).astype(BF16)


def _qkv_call(x, shift, scale, nw, w_qk, w_v_t, qnw, knw, cos, sin, bd):
    s, d = x.shape
    t = min(TOKEN_BLOCK, s)
    vr = ATT_HEAD_DIM + ATT_ONES_ROWS
    row = lambda i: (i, 0)
    head = lambda i: (0, i, 0)
    return pl.pallas_call(
        _qkv_kernel,
        grid=(s // t,),
        in_specs=[
            pl.BlockSpec((t, d), row), _full((1, d)), _full((1, d)), _full((1, d)),
            _full(w_qk.shape), _full(w_v_t.shape), _full((1, ATT_WIDTH)), _full((1, ATT_KV_WIDTH)),
            pl.BlockSpec((t, LANES), row), pl.BlockSpec((t, LANES), row), _full(bd.shape),
        ],
        out_specs=[
            pl.BlockSpec((t, d), row),
            pl.BlockSpec((ATT_HEADS, t, ATT_HEAD_DIM), head),
            pl.BlockSpec((ATT_KV_HEADS, t, ATT_HEAD_DIM), head),
            pl.BlockSpec((ATT_KV_HEADS, vr, t), lambda i: (0, 0, i)),
        ],
        out_shape=[
            jax.ShapeDtypeStruct((s, d), BF16),
            jax.ShapeDtypeStruct((ATT_HEADS, s, ATT_HEAD_DIM), BF16),
            jax.ShapeDtypeStruct((ATT_KV_HEADS, s, ATT_HEAD_DIM), BF16),
            jax.ShapeDtypeStruct((ATT_KV_HEADS, vr, s), BF16),
        ],
        compiler_params=_cparams("parallel"),
        name="qkv_proj",
    )(x, shift, scale, nw, w_qk, w_v_t, qnw, knw, cos, sin, bd)


def _hg_proj_kernel(layer, h_ref, w_ref, lb_ref, q_ref, kf_ref, gf_ref, kb_ref, gb_ref,
                    v_ref, og_ref):
    hb = h_ref[...]

    def proj(j):
        return jnp.dot(hb, w_ref[:, j * HG_WIDTH:(j + 1) * HG_WIDTH], preferred_element_type=F32)

    lb_raw = lb_ref[...]
    n_layers = lb_raw.shape[0]
    mx = lb_raw[0]
    for i in range(1, n_layers):
        mx = jnp.maximum(mx, lb_raw[i])
    ex = [jnp.exp(lb_raw[i] - mx) for i in range(n_layers)]
    den = ex[0]
    for i in range(1, n_layers):
        den = den + ex[i]
    lb = jnp.zeros_like(mx)
    for i in range(1, layer + 1):
        lb = lb + ex[i] / den

    def forget(z, lbd):
        f = lbd + (1.0 - lbd) * _sigmoid(z)
        return jnp.log(jnp.maximum(f, MIN_FORGET)), (1.0 - lbd) * _sigmoid(-z)

    q_ref[...] = _silu(proj(0))
    gf, kf = forget(proj(1), lb[0:1])
    gf_ref[...] = gf
    kf_ref[...] = kf
    gb, kb = forget(proj(2), lb[1:2])
    gb_ref[...] = gb
    kb_ref[...] = kb
    v_ref[...] = proj(3)
    og_ref[...] = _silu(proj(4))


def _hg_proj_call(layer, h, w_hg, lb_raw):
    s, d = h.shape
    t = min(TOKEN_BLOCK, s)
    row = lambda i: (i, 0)
    out = jax.ShapeDtypeStruct((s, HG_WIDTH), F32)
    return pl.pallas_call(
        functools.partial(_hg_proj_kernel, layer),
        grid=(s // t,),
        in_specs=[pl.BlockSpec((t, d), row), _full(w_hg.shape), _full(lb_raw.shape)],
        out_specs=[pl.BlockSpec((t, HG_WIDTH), row)] * 7,
        out_shape=[out] * 7,
        compiler_params=_cparams("parallel"),
        name="hg_proj",
    )(h, w_hg, lb_raw)


def _gate_kernel(h_ref, w_ref, g_ref):
    g_ref[...] = _sigmoid(jnp.dot(h_ref[...], w_ref[...], preferred_element_type=F32)).astype(BF16)


def _gate_call(h, w_gate):
    s, d = h.shape
    t = min(TOKEN_BLOCK, s)
    n = w_gate.shape[1]
    return pl.pallas_call(
        _gate_kernel,
        grid=(s // t,),
        in_specs=[pl.BlockSpec((t, d), lambda i: (i, 0)), _full(w_gate.shape)],
        out_specs=pl.BlockSpec((t, n), lambda i: (i, 0)),
        out_shape=jax.ShapeDtypeStruct((s, n), BF16),
        compiler_params=_cparams("parallel"),
        name="gate_proj",
    )(h, w_gate)


def _attn_kernel(q_ref, k_ref, vt_ref, o_ref, m_ref, acc_ref):
    g, tq, hd = q_ref.shape
    s = k_ref.shape[1]
    tk = min(ATT_KV_BLOCK, s)
    heads_per_part = g // ATT_ROW_PARTS
    cols = heads_per_part * tq
    acc_ref[...] = jnp.zeros(acc_ref.shape, F32)

    def scores(ks, r):
        q = q_ref[r * heads_per_part:(r + 1) * heads_per_part].reshape(cols, hd)
        return lax.dot_general(ks, q, (((1,), (1,)), ((), ())), preferred_element_type=F32)

    ks0 = k_ref[0, pl.ds(0, min(ATT_INIT_KEYS, tk)), :]
    for r in range(ATT_ROW_PARTS):
        m_ref[:, pl.ds(r * cols, cols)] = jnp.max(scores(ks0, r), axis=0, keepdims=True)

    def body(j, carry):
        start = pl.multiple_of(j * tk, tk)
        ks = k_ref[0, pl.ds(start, tk), :]
        vt = vt_ref[0, :, pl.ds(start, tk)]
        parts = [pl.ds(r * cols, cols) for r in range(ATT_ROW_PARTS)]
        sts = [scores(ks, r) for r in range(ATT_ROW_PARTS)]
        pvs, m_news, lag = [], [], []
        for sl, st in zip(parts, sts):
            m_old = m_ref[:, sl]
            m_blk = jnp.max(st, axis=0, keepdims=True)
            p = jnp.exp2(st - m_old).astype(BF16)
            pvs.append(jnp.dot(vt, p, preferred_element_type=F32))
            m_news.append(jnp.maximum(m_old, m_blk))
            lag.append(jnp.max(m_blk - m_old))
        worst_lag = functools.reduce(jnp.maximum, lag)
        safe = worst_lag <= ATT_MAX_LAG

        @pl.when(safe)
        def _():
            for sl, pv, m_new in zip(parts, pvs, m_news):
                acc_ref[:, sl] = (acc_ref[:, sl] + pv) * jnp.exp2(m_ref[:, sl] - m_new)
                m_ref[:, sl] = m_new

        @pl.when(jnp.logical_not(safe))
        def _():
            for r, sl in enumerate(parts):
                st = scores(ks, r)
                m_old = m_ref[:, sl]
                m_new = jnp.maximum(m_old, jnp.max(st, axis=0, keepdims=True))
                p = jnp.exp2(st - m_new).astype(BF16)
                acc_ref[:, sl] = (jnp.exp2(m_old - m_new) * acc_ref[:, sl]
                                  + jnp.dot(vt, p, preferred_element_type=F32))
                m_ref[:, sl] = m_new
        return carry

    lax.fori_loop(0, s // tk, body, 0)
    acc = acc_ref[...]
    o_t = acc[:hd] / acc[hd:hd + 1]
    for i in range(g):
        o_ref[:, i * hd:(i + 1) * hd] = o_t[:, i * tq:(i + 1) * tq].T.astype(BF16)


def _attn_call(q, k, vt):
    _, s, hd = q.shape
    vr = vt.shape[1]
    tq = min(ATT_Q_BLOCK, s)
    return pl.pallas_call(
        _attn_kernel,
        grid=(ATT_KV_HEADS, s // tq),
        in_specs=[
            pl.BlockSpec((ATT_GROUP, tq, hd), lambda h, i: (h, i, 0)),
            pl.BlockSpec((1, s, hd), lambda h, i: (h, 0, 0)),
            pl.BlockSpec((1, vr, s), lambda h, i: (h, 0, 0)),
        ],
        out_specs=pl.BlockSpec((tq, ATT_GROUP * hd), lambda h, i: (i, h)),
        out_shape=jax.ShapeDtypeStruct((s, ATT_HEADS * hd), BF16),
        scratch_shapes=[pltpu.VMEM((1, ATT_GROUP * tq), F32),
                        pltpu.VMEM((vr, ATT_GROUP * tq), F32)],
        compiler_params=_cparams("parallel", "parallel"),
        name="attention",
    )(q, k, vt)


def _chunk_cumsum(g, reverse):
    c = g.shape[0]
    row = lax.broadcasted_iota(I32, g.shape, 0)
    x = g
    step = 1
    while step < c:
        if not reverse:
            x = x + jnp.where(row >= step, pltpu.roll(x, step, 0), 0.0)
        else:
            x = x + jnp.where(row < c - step, pltpu.roll(x, c - step, 0), 0.0)
        step *= 2
    return x


def _boundary_rows(b_ref, half, idx):
    c, dd = b_ref.shape
    blk = 2 * half
    pieces = []
    if blk >= 8:
        for s0 in range(0, c, blk):
            pieces.append(jnp.broadcast_to(b_ref[pl.ds(s0 + idx, 1), :], (blk, dd)))
    else:
        sub = lax.broadcasted_iota(I32, (8, dd), 0)
        for v0 in range(0, c, 8):
            piece = jnp.broadcast_to(b_ref[pl.ds(v0 + idx, 1), :], (8, dd))
            for s0 in range(v0 + blk, v0 + 8, blk):
                row = jnp.broadcast_to(b_ref[pl.ds(s0 + idx, 1), :], (8, dd))
                piece = jnp.where(sub >= s0 - v0, row, piece)
            pieces.append(piece)
    return jnp.concatenate(pieces, axis=0)


def _hg_scan_body(reverse, q_ref, k_ref, v_ref, g_ref, mask_ref, st_ref, b_ref):
    c = q_ref.shape[0]
    heads = [slice(h * HG_DIM, (h + 1) * HG_DIM) for h in range(q_ref.shape[1] // HG_DIM)]
    nt = (((1,), (1,)), ((), ()))
    q = q_ref[...]
    k = k_ref[...]
    vb = v_ref[...].astype(BF16)
    b = _chunk_cumsum(g_ref[...] * LOG2_E, reverse)
    b_ref[...] = b
    b_edge = b_ref[pl.ds(c - 1 if not reverse else 0, 1), :]
    qh = (q * jnp.exp2(b)).astype(BF16)
    kh = (k * jnp.exp2(b_edge - b)).astype(BF16)
    st_decay = jnp.exp2(b_edge)
    outs = []
    for h, sl in enumerate(heads):
        st = st_ref[h]
        outs.append(lax.dot_general(qh[:, sl], st.astype(BF16), nt, preferred_element_type=F32))
        st_ref[h] = st * st_decay[:, sl] + lax.dot_general(
            vb[:, sl], kh[:, sl], (((0,), (0,)), ((), ())), preferred_element_type=F32)

    qb = q.astype(BF16)
    kb = k.astype(BF16)
    scores = [mask_ref[0] * lax.dot_general(qb[:, sl], kb[:, sl], nt, preferred_element_type=F32)
              for sl in heads]
    half = c // 2
    level = 1
    while half >= 1:
        d = b - _boundary_rows(b_ref, half, half - 1 if not reverse else half)
        qt = (q * jnp.exp2(jnp.minimum(d, 0.0))).astype(BF16)
        kt = (k * jnp.exp2(jnp.minimum(-d, 0.0))).astype(BF16)
        mask = mask_ref[level]
        scores = [a + mask * lax.dot_general(qt[:, sl], kt[:, sl], nt, preferred_element_type=F32)
                  for a, sl in zip(scores, heads)]
        half //= 2
        level += 1
    return [o + jnp.dot(a.astype(BF16), vb[:, sl], preferred_element_type=F32)
            for o, a, sl in zip(outs, scores, heads)]


def _hg_fwd_kernel(q_ref, k_ref, v_ref, g_ref, mask_ref, o_ref, st_ref, b_ref):
    @pl.when(pl.program_id(0) == 0)
    def _():
        st_ref[...] = jnp.zeros_like(st_ref)

    outs = _hg_scan_body(False, q_ref, k_ref, v_ref, g_ref, mask_ref, st_ref, b_ref)
    o_ref[...] = jnp.concatenate(outs, axis=1)


def _hg_bwd_kernel(q_ref, k_ref, v_ref, g_ref, mask_ref, of_ref, og_ref, nw_ref, o_ref, st_ref,
                   b_ref):
    @pl.when(pl.program_id(0) == 0)
    def _():
        st_ref[...] = jnp.zeros_like(st_ref)

    outs = _hg_scan_body(True, q_ref, k_ref, v_ref, g_ref, mask_ref, st_ref, b_ref)
    normed = []
    for h, o_b in enumerate(outs):
        o = of_ref[:, h * HG_DIM:(h + 1) * HG_DIM] + o_b
        ms = jnp.mean(o * o, axis=-1, keepdims=True)
        normed.append(o * lax.rsqrt(ms + NORM_EPS) * nw_ref[...])
    o_ref[...] = (jnp.concatenate(normed, axis=1) * og_ref[...]).astype(BF16)


def _level_masks(c, reverse):
    ii = np.arange(c)[:, None]
    jj = np.arange(c)[None, :]
    masks = [ii == jj]
    half = c // 2
    while half >= 1:
        blk = 2 * half
        same = (ii // blk) == (jj // blk)
        q_hi, k_hi = (ii % blk) >= half, (jj % blk) >= half
        masks.append(same & (q_hi & ~k_hi if not reverse else ~q_hi & k_hi))
        half //= 2
    return jnp.asarray(np.stack(masks).astype(np.float32))


def _hg_call(q, kf, gf, kb, gb, v, og, nw):
    s = q.shape[0]
    c = min(HG_CHUNK, s)
    n = s // c
    mask_f = _level_masks(c, False)
    mask_b = _level_masks(c, True)
    fwd = lambda i: (i, 0)
    bwd = lambda i: (n - 1 - i, 0)
    blk = lambda m: pl.BlockSpec((c, HG_WIDTH), m)
    scratch = [pltpu.VMEM((HG_HEADS, HG_DIM, HG_DIM), F32), pltpu.VMEM((c, HG_WIDTH), F32)]
    o_fwd = pl.pallas_call(
        _hg_fwd_kernel,
        grid=(n,),
        in_specs=[blk(fwd)] * 4 + [_full(mask_f.shape)],
        out_specs=blk(fwd),
        out_shape=jax.ShapeDtypeStruct((s, HG_WIDTH), F32),
        scratch_shapes=scratch,
        compiler_params=_cparams("arbitrary"),
        name="hg_scan_fwd",
    )(q, kf, v, gf, mask_f)
    return pl.pallas_call(
        _hg_bwd_kernel,
        grid=(n,),
        in_specs=[blk(bwd)] * 4 + [_full(mask_b.shape), blk(bwd), blk(bwd), _full((1, HG_DIM))],
        out_specs=blk(bwd),
        out_shape=jax.ShapeDtypeStruct((s, HG_WIDTH), BF16),
        scratch_shapes=scratch,
        compiler_params=_cparams("arbitrary"),
        name="hg_scan_bwd",
    )(q, kb, v, gb, mask_b, o_fwd, og, nw)


def _merge_kernel(x_ref, oa_ref, oh_ref, g_ref, wa_ref, wh_ref, wo_ref, g1_ref, sh_ref, sc_ref,
                  nw_ref, wr_ref, xo_ref, h_ref, aff_ref):
    d = x_ref.shape[1]
    ya = jnp.dot(oa_ref[...], wa_ref[...], preferred_element_type=F32)
    yh = jnp.dot(oh_ref[...], wh_ref[...], preferred_element_type=F32)
    merged = g_ref[:, :d].astype(F32) * ya + g_ref[:, d:].astype(F32) * yh
    y = jnp.dot(merged.astype(BF16), wo_ref[...], preferred_element_type=F32)
    xn = x_ref[...] + g1_ref[...] * y
    xo_ref[...] = xn
    h = _rms_mod(xn, nw_ref[...], sc_ref[...], sh_ref[...])
    h_ref[...] = h.astype(BF16)
    logits = lax.dot_general(wr_ref[...], h, (((1,), (1,)), ((), ())), precision=HIGHEST,
                             preferred_element_type=F32)
    ex = jnp.exp(logits - jnp.max(logits, axis=0, keepdims=True))
    aff_ref[...] = ex / jnp.sum(ex, axis=0, keepdims=True)


def _merge_call(x, o_att, o_hg, g, w_ba, w_bh, w_out, gate1, shift2, scale2, nw2, w_router_t):
    s, d = x.shape
    t = min(TOKEN_BLOCK, s)
    row = lambda i: (i, 0)
    return pl.pallas_call(
        _merge_kernel,
        grid=(s // t,),
        in_specs=[
            pl.BlockSpec((t, d), row), pl.BlockSpec((t, ATT_WIDTH), row),
            pl.BlockSpec((t, HG_WIDTH), row), pl.BlockSpec((t, 2 * d), row),
            _full(w_ba.shape), _full(w_bh.shape), _full(w_out.shape),
            _full((1, d)), _full((1, d)), _full((1, d)), _full((1, d)), _full(w_router_t.shape),
        ],
        out_specs=[pl.BlockSpec((t, d), row), pl.BlockSpec((t, d), row),
                   pl.BlockSpec((N_EXPERTS, t), lambda i: (0, i))],
        out_shape=[jax.ShapeDtypeStruct((s, d), F32), jax.ShapeDtypeStruct((s, d), BF16),
                   jax.ShapeDtypeStruct((N_EXPERTS, s), F32)],
        compiler_params=_cparams("parallel"),
        name="merge_router",
    )(x, o_att, o_hg, g, w_ba, w_bh, w_out, gate1, shift2, scale2, nw2, w_router_t)


def _select_kernel(cap, aff_ref, tri_ref, pos_ref, starts_ref):
    n_exp, s = aff_ref.shape
    t = tri_ref.shape[0]
    n_blk = s // t
    aff = aff_ref[...]

    def bit_step(i, thr_bits):
        cand = thr_bits | lax.shift_left(jnp.int32(1), jnp.int32(30) - i)
        cnt = jnp.sum((aff >= pltpu.bitcast(cand, F32)).astype(F32), axis=1, keepdims=True)
        return jnp.where(cnt >= cap, cand, thr_bits)

    thr = pltpu.bitcast(lax.fori_loop(0, 31, bit_step, jnp.zeros((n_exp, 1), I32)), F32)
    n_gt = jnp.sum((aff > thr).astype(F32), axis=1, keepdims=True)
    need_eq = cap - n_gt
    tri = tri_ref[...]
    blk_lane = lax.broadcasted_iota(I32, starts_ref.shape, 1)

    starts_ref[...] = jnp.zeros(starts_ref.shape, I32)

    def blk_step(j, carry):
        c_eq, c_sel = carry
        start = pl.multiple_of(j * t, t)
        bt = aff_ref[:, pl.ds(start, t)]
        eq = bt == thr
        eq_f = eq.astype(BF16)
        eq_excl = c_eq + jnp.dot(eq_f, tri, preferred_element_type=F32) - eq_f.astype(F32)
        sel = (bt > thr) | (eq & (eq_excl < need_eq))
        sel_f = sel.astype(BF16)
        sel_incl = c_sel + jnp.dot(sel_f, tri, preferred_element_type=F32)
        pos_ref[:, pl.ds(start, t)] = jnp.where(sel, sel_incl - 1.0, -1.0).astype(I32)
        starts_ref[...] = jnp.where(blk_lane == j, c_sel.astype(I32), starts_ref[...])
        return (c_eq + jnp.sum(eq_f.astype(F32), axis=1, keepdims=True),
                c_sel + jnp.sum(sel_f.astype(F32), axis=1, keepdims=True))

    zero = jnp.zeros((n_exp, 1), F32)
    _, c_sel = lax.fori_loop(0, n_blk, blk_step, (zero, zero))
    starts_ref[...] = jnp.where(blk_lane >= n_blk, c_sel.astype(I32), starts_ref[...])


def _select_call(aff_t, cap):
    n_exp, s = aff_t.shape
    t = min(ROUTE_BLOCK, s)
    ii = np.arange(t)
    tri = jnp.asarray((ii[:, None] <= ii[None, :]).astype(np.float32), dtype=BF16)
    return pl.pallas_call(
        functools.partial(_select_kernel, float(cap)),
        grid=(1,),
        in_specs=[_full((n_exp, s)), _full((t, t))],
        out_specs=[_full((n_exp, s)), _full((n_exp, LANES))],
        out_shape=[jax.ShapeDtypeStruct((n_exp, s), I32), jax.ShapeDtypeStruct((n_exp, LANES), I32)],
        compiler_params=_cparams("arbitrary"),
        name="route_select",
    )(aff_t, tri)


def _gather_kernel(cap, per, starts_ref, pos_ref, h_ref, xe_ref, acc_ref):
    e = pl.program_id(0)
    b = pl.program_id(1)
    t = h_ref.shape[0]

    @pl.when(b == 0)
    def _():
        acc_ref[...] = jnp.zeros_like(acc_ref)

    lo = starts_ref[e, b * per]
    hi = starts_ref[e, (b + 1) * per]
    c0 = lo // ROW_CHUNK
    pos = pos_ref[pl.ds(e, 1), :]
    slot = lax.broadcasted_iota(I32, (ROW_CHUNK, t), 0)
    for kk in range(t // ROW_CHUNK + 1):
        c = c0 + kk

        @pl.when((hi > lo) & (c * ROW_CHUNK < hi) & (c < cap // ROW_CHUNK))
        def _():
            base = pl.multiple_of(c * ROW_CHUNK, ROW_CHUNK)
            onehot = (pos - base == slot).astype(BF16)
            acc_ref[pl.ds(base, ROW_CHUNK), :] += jnp.dot(onehot, h_ref[...],
                                                          preferred_element_type=F32)

    @pl.when(b == pl.num_programs(1) - 1)
    def _():
        xe_ref[0] = acc_ref[...].astype(BF16)


def _gather_call(starts, pos_t, h, cap):
    s, d = h.shape
    t = min(GATHER_BLOCK, s)
    return pl.pallas_call(
        functools.partial(_gather_kernel, cap, t // min(ROUTE_BLOCK, s)),
        grid_spec=pltpu.PrefetchScalarGridSpec(
            num_scalar_prefetch=1,
            grid=(N_EXPERTS, s // t),
            in_specs=[pl.BlockSpec((N_EXPERTS, t), lambda e, b, st: (0, b)),
                      pl.BlockSpec((t, d), lambda e, b, st: (b, 0))],
            out_specs=pl.BlockSpec((1, cap, d), lambda e, b, st: (e, 0, 0)),
            scratch_shapes=[pltpu.VMEM((cap, d), F32)],
        ),
        out_shape=jax.ShapeDtypeStruct((N_EXPERTS, cap, d), BF16),
        compiler_params=_cparams("parallel", "arbitrary"),
        name="expert_gather",
    )(starts, pos_t, h)


def _ffn_kernel(xe_ref, wg_ref, wu_ref, wd_ref, ye_ref, acc_ref):
    f = pl.program_id(1)
    last = pl.num_programs(1) - 1
    rows = xe_ref.shape[1] // FF_ROW_PARTS
    wg = wg_ref[0, 0].astype(BF16)
    wu = wu_ref[0, 0].astype(BF16)
    wd = wd_ref[0, 0].astype(BF16)
    parts = [pl.ds(r * rows, rows) for r in range(FF_ROW_PARTS)]
    gate_up = []
    for sl in parts:
        x = xe_ref[0, sl, :]
        gate_up.append((jnp.dot(x, wg, preferred_element_type=F32),
                        jnp.dot(x, wu, preferred_element_type=F32)))
    hids = [(_silu(a) * u).astype(BF16) for a, u in gate_up]

    def down(hid):
        return jnp.dot(hid, wd, preferred_element_type=F32)

    @pl.when(f == 0)
    def _():
        for sl, hid in zip(parts, hids):
            acc_ref[sl, :] = down(hid)

    @pl.when((f > 0) & (f < last))
    def _():
        for sl, hid in zip(parts, hids):
            acc_ref[sl, :] += down(hid)

    @pl.when(f == last)
    def _():
        for sl, hid in zip(parts, hids):
            ye_ref[0, sl, :] = (acc_ref[sl, :] + down(hid)).astype(BF16)


def _ffn_call(layer, xe, w_gate, w_up, w_down):
    n_exp, cap, d = xe.shape
    ff = w_gate.shape[3]
    fb = min(FF_BLOCK, ff)
    assert ff // fb >= 2, "the first and last hidden blocks take different accumulate paths"
    return pl.pallas_call(
        _ffn_kernel,
        grid=(n_exp, ff // fb),
        in_specs=[pl.BlockSpec((1, cap, d), lambda e, f: (e, 0, 0)),
                  pl.BlockSpec((1, 1, d, fb), lambda e, f: (layer, e, 0, f)),
                  pl.BlockSpec((1, 1, d, fb), lambda e, f: (layer, e, 0, f)),
                  pl.BlockSpec((1, 1, fb, d), lambda e, f: (layer, e, f, 0))],
        out_specs=pl.BlockSpec((1, cap, d), lambda e, f: (e, 0, 0)),
        out_shape=jax.ShapeDtypeStruct((n_exp, cap, d), BF16),
        scratch_shapes=[pltpu.VMEM((cap, d), F32)],
        compiler_params=_cparams("parallel", "arbitrary"),
        name="expert_ffn",
    )(xe, w_gate, w_up, w_down)


def _combine_kernel(n_chunks, n_cand, starts_ref, pos_ref, w_ref, x_ref, g2_ref, ye_ref, xo_ref,
                    acc_ref, buf_ref, sem_ref):
    b = pl.program_id(0)
    t = x_ref.shape[0]
    n_exp = pos_ref.shape[1]

    def first_chunk(e):
        return starts_ref[e, b] // ROW_CHUNK

    def needed(e, kk):
        lo = starts_ref[e, b]
        hi = starts_ref[e, b + 1]
        c = first_chunk(e) + kk
        return (hi > lo) & (c * ROW_CHUNK < hi) & (c < n_chunks)

    def chunk_copy(e, kk, slot):
        c = jnp.minimum(first_chunk(e) + kk, n_chunks - 1)
        rows = pl.ds(pl.multiple_of(c * ROW_CHUNK, ROW_CHUNK), ROW_CHUNK)
        return pltpu.make_async_copy(ye_ref.at[e, rows, :], buf_ref.at[slot, kk],
                                     sem_ref.at[slot, kk])

    def start_expert(e, slot):
        for kk in range(n_cand):
            @pl.when(needed(e, kk))
            def _():
                chunk_copy(e, kk, slot).start()

    def wait_expert(e, slot):
        for kk in range(n_cand):
            @pl.when(needed(e, kk))
            def _():
                chunk_copy(e, kk, slot).wait()

    acc_ref[...] = jnp.zeros_like(acc_ref)
    start_expert(0, 0)
    lane = lax.broadcasted_iota(I32, pos_ref.shape, 1)
    row_slot = lax.broadcasted_iota(I32, (t, ROW_CHUNK), 1)

    def expert_step(e, carry):
        slot = lax.rem(e, 2)

        @pl.when(e + 1 < n_exp)
        def _():
            start_expert(e + 1, 1 - slot)

        wait_expert(e, slot)
        pos = jnp.sum(jnp.where(lane == e, pos_ref[...], 0), axis=1, keepdims=True)
        wgt = jnp.sum(jnp.where(lane == e, w_ref[...], 0.0), axis=1, keepdims=True)
        for kk in range(n_cand):
            @pl.when(needed(e, kk))
            def _():
                base = (first_chunk(e) + kk) * ROW_CHUNK
                onehot = (pos - base == row_slot).astype(BF16)
                acc_ref[...] += wgt * jnp.dot(onehot, buf_ref[slot, kk],
                                              preferred_element_type=F32)
        return carry

    lax.fori_loop(0, n_exp, expert_step, 0)
    xo_ref[...] = x_ref[...] + g2_ref[...] * acc_ref[...]


def _combine_call(starts, pos_tok, w_tok, x, gate2, ye):
    s, d = x.shape
    n_exp, cap, _ = ye.shape
    t = min(ROUTE_BLOCK, s)
    n_chunks = cap // ROW_CHUNK
    n_cand = min(t // ROW_CHUNK + 1, n_chunks)
    return pl.pallas_call(
        functools.partial(_combine_kernel, n_chunks, n_cand),
        grid_spec=pltpu.PrefetchScalarGridSpec(
            num_scalar_prefetch=1,
            grid=(s // t,),
            in_specs=[pl.BlockSpec((t, n_exp), lambda b, st: (b, 0)),
                      pl.BlockSpec((t, n_exp), lambda b, st: (b, 0)),
                      pl.BlockSpec((t, d), lambda b, st: (b, 0)),
                      pl.BlockSpec((1, d), lambda b, st: (0, 0)),
                      pl.BlockSpec(memory_space=pl.ANY)],
            out_specs=pl.BlockSpec((t, d), lambda b, st: (b, 0)),
            scratch_shapes=[pltpu.VMEM((t, d), F32),
                            pltpu.VMEM((2, n_cand, ROW_CHUNK, d), BF16),
                            pltpu.SemaphoreType.DMA((2, n_cand))],
        ),
        out_shape=jax.ShapeDtypeStruct((s, d), F32),
        compiler_params=_cparams("arbitrary"),
        name="expert_combine",
    )(starts, pos_tok, w_tok, x, gate2, ye)


def _rope_tables(seq_len):
    rows = seq_len // GRID_W
    row_id = jnp.repeat(jnp.arange(rows, dtype=F32), GRID_W)
    col_id = jnp.tile(jnp.arange(GRID_W, dtype=F32), rows)
    inv_freq = ROPE_THETA ** (-jnp.arange(0, ROPE_AXIS_DIM, 2, dtype=F32) / ROPE_AXIS_DIM)
    ang = jnp.stack([row_id[:, None] * inv_freq, col_id[:, None] * inv_freq], axis=1)
    cos, sin = jnp.cos(ang), jnp.sin(ang)
    cos_h = jnp.concatenate([cos, cos], axis=-1).reshape(seq_len, ATT_HEAD_DIM)
    sin_h = jnp.concatenate([-sin, sin], axis=-1).reshape(seq_len, ATT_HEAD_DIM)
    reps = LANES // ATT_HEAD_DIM
    return jnp.tile(cos_h, (1, reps)), jnp.tile(sin_h, (1, reps))


def _head_mean_operator():
    ii = np.arange(ATT_WIDTH)
    same = (ii[:, None] // ATT_HEAD_DIM) == (ii[None, :] // ATT_HEAD_DIM)
    return jnp.asarray(same.astype(np.float32) / ATT_HEAD_DIM)


def kernel(x, c, ada_w, ada_b, norm_mix_w, norm_ffn_w, w_in, q_norm_w, k_norm_w, hg_lower_bounds,
           hg_norm_w, w_branch_att, w_branch_hg, w_out, w_router, w_exp_gate, w_exp_up, w_exp_down):
    batch, s, d = x.shape
    assert batch == 1, "kernel is written for a single sequence"
    n_layers = ada_w.shape[0]
    cap = EC_CAPACITY_FACTOR * s // N_EXPERTS
    assert cap % ROW_CHUNK == 0 and s % min(ROUTE_BLOCK, s) == 0

    cos, sin = _rope_tables(s)
    bd = _head_mean_operator()
    mods = _ada_mod(c, ada_w, ada_b)
    qkv_w = ATT_WIDTH + 2 * ATT_KV_WIDTH
    hg_w = 5 * HG_WIDTH
    xs = x[0]
    for l in range(n_layers):
        shift1, scale1, gate1, shift2, scale2, gate2 = [
            mods[l, i * d:(i + 1) * d].reshape(1, d) for i in range(6)]
        qk_w = ATT_WIDTH + ATT_KV_WIDTH
        h, q, k, v = _qkv_call(
            xs, shift1, scale1, norm_mix_w[l].reshape(1, d), w_in[l, :, :qk_w].astype(BF16),
            w_in[l, :, qk_w:qkv_w].T.astype(BF16),
            jnp.tile(q_norm_w[l], ATT_HEADS).reshape(1, ATT_WIDTH),
            jnp.tile(k_norm_w[l], ATT_KV_HEADS).reshape(1, ATT_KV_WIDTH), cos, sin, bd)
        hq, kf, gf, kb, gb, hv, og = _hg_proj_call(
            l, h, w_in[l, :, qkv_w:qkv_w + hg_w].astype(BF16), hg_lower_bounds)
        g = _gate_call(h, w_in[l, :, qkv_w + hg_w:].astype(BF16))
        o_att = _attn_call(q, k, v)
        o_hg = _hg_call(hq, kf, gf, kb, gb, hv, og, hg_norm_w[l].reshape(1, HG_DIM))
        xs, h2, aff_t = _merge_call(
            xs, o_att, o_hg, g, w_branch_att[l].astype(BF16), w_branch_hg[l].astype(BF16),
            w_out[l].astype(BF16), gate1, shift2, scale2, norm_ffn_w[l].reshape(1, d),
            w_router[l].T)
        pos_t, starts = _select_call(aff_t, cap)
        xe = _gather_call(starts, pos_t, h2, cap)
        ye = _ffn_call(l, xe, w_exp_gate, w_exp_up, w_exp_down)
        xs = _combine_call(starts, pos_t.T, aff_t.T, xs, gate2, ye)
    return xs[None]
```

```python
import functools

import numpy as np
import jax
import jax.numpy as jnp
from jax import lax
from jax.experimental import pallas as pl
from jax.experimental.pallas import tpu as pltpu

F32 = jnp.float32
BF16 = jnp.bfloat16
I32 = jnp.int32
HIGHEST = lax.Precision.HIGHEST

GRID_W = 64
ATT_HEADS = 8
ATT_KV_HEADS = 2
ATT_GROUP = ATT_HEADS // ATT_KV_HEADS
ATT_HEAD_DIM = 64
ATT_WIDTH = ATT_HEADS * ATT_HEAD_DIM
ATT_KV_WIDTH = ATT_KV_HEADS * ATT_HEAD_DIM
ROPE_AXIS_DIM = ATT_HEAD_DIM // 2
ROPE_THETA = 10000.0
HG_HEADS = 4
HG_DIM = 128
HG_WIDTH = HG_HEADS * HG_DIM
MIN_FORGET = 1e-6
N_EXPERTS = 16
EC_CAPACITY_FACTOR = 2
NORM_EPS = 1e-6
LOG2_E = 1.4426950408889634

LANES = 128
VMEM_LIMIT_BYTES = 56 * 1024 * 1024

TOKEN_BLOCK = 512
ATT_Q_BLOCK = 512
ATT_KV_BLOCK = 2048
ATT_ROW_PARTS = 4
ATT_ONES_ROWS = 16
ATT_MAX_LAG = 64.0
ATT_INIT_KEYS = 256
HG_CHUNK = 128
HG_SUB = 16
ROUTE_BLOCK = 512
GATHER_BLOCK = 1024
ROW_CHUNK_LOG2 = 8
ROW_CHUNK = 1 << ROW_CHUNK_LOG2
FF_BLOCK = 512
FF_ROW_PARTS = 2


def _cparams(*sem):
    return pltpu.CompilerParams(dimension_semantics=sem, vmem_limit_bytes=VMEM_LIMIT_BYTES)


def _sigmoid(x):
    return 1.0 / (1.0 + jnp.exp(-x))


def _silu(x):
    return x * _sigmoid(x)


def _full(shape):
    return pl.BlockSpec(shape, lambda *_: (0,) * len(shape))


def _ada_kernel(c_ref, w_ref, b_ref, o_ref):
    ca = _silu(c_ref[...])
    o_ref[0] = jnp.dot(ca, w_ref[0], precision=HIGHEST, preferred_element_type=F32) + b_ref[0]


def _ada_mod(c, ada_w, ada_b):
    n_layers, d, w6 = ada_w.shape
    col = 1536
    c8 = jnp.broadcast_to(c, (8, d))
    out = pl.pallas_call(
        _ada_kernel,
        grid=(n_layers, w6 // col),
        in_specs=[
            pl.BlockSpec((8, d), lambda l, j: (0, 0)),
            pl.BlockSpec((1, d, col), lambda l, j: (l, 0, j)),
            pl.BlockSpec((1, 1, col), lambda l, j: (l, 0, j)),
        ],
        out_specs=pl.BlockSpec((1, 8, col), lambda l, j: (l, 0, j)),
        out_shape=jax.ShapeDtypeStruct((n_layers, 8, w6), F32),
        compiler_params=_cparams("parallel", "parallel"),
        name="ada_mod",
    )(c8, ada_w, ada_b.reshape(n_layers, 1, w6))
    return out[:, 0, :]


def _rms_mod(x, nw, scale, shift):
    ms = jnp.mean(x * x, axis=-1, keepdims=True)
    return (x * lax.rsqrt(ms + NORM_EPS) * nw) * (1.0 + scale) + shift


def _qkv_kernel(x_ref, sh_ref, sc_ref, nw_ref, w_ref, wvt_ref, qnw_ref, knw_ref, cos_ref,
                sin_ref, bd_ref, h_ref, q_ref, k_ref, vt_ref):
    h = _rms_mod(x_ref[...], nw_ref[...], sc_ref[...], sh_ref[...])
    hb = h.astype(BF16)
    h_ref[...] = hb
    proj = jnp.dot(hb, w_ref[...], preferred_element_type=F32)
    aq = proj[:, :ATT_WIDTH]
    ak = proj[:, ATT_WIDTH:]
    avt = lax.dot_general(wvt_ref[...], hb, (((1,), (1,)), ((), ())),
                          preferred_element_type=F32)
    bd = bd_ref[...]
    qms = jnp.dot(aq * aq, bd, precision=HIGHEST, preferred_element_type=F32)
    kms = jnp.dot(ak * ak, bd[:ATT_KV_WIDTH, :ATT_KV_WIDTH], precision=HIGHEST,
                  preferred_element_type=F32)
    qn = aq * lax.rsqrt(qms + NORM_EPS) * qnw_ref[...]
    kn = ak * lax.rsqrt(kms + NORM_EPS) * knw_ref[...]
    cos = cos_ref[...]
    sin = sin_ref[...]
    lane = lax.broadcasted_iota(I32, cos.shape, 1)
    first_half = (lane % ROPE_AXIS_DIM) < (ROPE_AXIS_DIM // 2)

    def rope(xs):
        up = pltpu.roll(xs, LANES - ROPE_AXIS_DIM // 2, 1)
        dn = pltpu.roll(xs, ROPE_AXIS_DIM // 2, 1)
        return xs * cos + jnp.where(first_half, up, dn) * sin

    q_scale = ATT_HEAD_DIM ** -0.5 * LOG2_E
    for j in range(ATT_WIDTH // LANES):
        qr = rope(qn[:, j * LANES:(j + 1) * LANES]) * q_scale
        q_ref[2 * j] = qr[:, :ATT_HEAD_DIM].astype(BF16)
        q_ref[2 * j + 1] = qr[:, ATT_HEAD_DIM:].astype(BF16)
    kr = rope(kn)
    ones = jnp.ones((ATT_ONES_ROWS, avt.shape[1]), F32)
    for j in range(ATT_KV_HEADS):
        k_ref[j] = kr[:, j * ATT_HEAD_DIM:(j + 1) * ATT_HEAD_DIM].astype(BF16)
        vj = avt[j * ATT_HEAD_DIM:(j + 1) * ATT_HEAD_DIM]
        vt_ref[j] = jnp.concatenate([vj, ones], axis=0).astype(BF16)


def _qkv_call(x, shift, scale, nw, w_qk, w_v_t, qnw, knw, cos, sin, bd):
    s, d = x.shape
    t = min(TOKEN_BLOCK, s)
    vr = ATT_HEAD_DIM + ATT_ONES_ROWS
    row = lambda i: (i, 0)
    head = lambda i: (0, i, 0)
    return pl.pallas_call(
        _qkv_kernel,
        grid=(s // t,),
        in_specs=[
            pl.BlockSpec((t, d), row), _full((1, d)), _full((1, d)), _full((1, d)),
            _full(w_qk.shape), _full(w_v_t.shape), _full((1, ATT_WIDTH)), _full((1, ATT_KV_WIDTH)),
            pl.BlockSpec((t, LANES), row), pl.BlockSpec((t, LANES), row), _full(bd.shape),
        ],
        out_specs=[
            pl.BlockSpec((t, d), row),
            pl.BlockSpec((ATT_HEADS, t, ATT_HEAD_DIM), head),
            pl.BlockSpec((ATT_KV_HEADS, t, ATT_HEAD_DIM), head),
            pl.BlockSpec((ATT_KV_HEADS, vr, t), lambda i: (0, 0, i)),
        ],
        out_shape=[
            jax.ShapeDtypeStruct((s, d), BF16),
            jax.ShapeDtypeStruct((ATT_HEADS, s, ATT_HEAD_DIM), BF16),
            jax.ShapeDtypeStruct((ATT_KV_HEADS, s, ATT_HEAD_DIM), BF16),
            jax.ShapeDtypeStruct((ATT_KV_HEADS, vr, s), BF16),
        ],
        compiler_params=_cparams("parallel"),
        name="qkv_proj",
    )(x, shift, scale, nw, w_qk, w_v_t, qnw, knw, cos, sin, bd)


def _hg_proj_kernel(layer, h_ref, w_ref, lb_ref, q_ref, kf_ref, gf_ref, kb_ref, gb_ref,
                    v_ref, og_ref):
    hb = h_ref[...]

    def proj(j):
        return jnp.dot(hb, w_ref[:, j * HG_WIDTH:(j + 1) * HG_WIDTH], preferred_element_type=F32)

    lb_raw = lb_ref[...]
    n_layers = lb_raw.shape[0]
    mx = lb_raw[0]
    for i in range(1, n_layers):
        mx = jnp.maximum(mx, lb_raw[i])
    ex = [jnp.exp(lb_raw[i] - mx) for i in range(n_layers)]
    den = ex[0]
    for i in range(1, n_layers):
        den = den + ex[i]
    lb = jnp.zeros_like(mx)
    for i in range(1, layer + 1):
        lb = lb + ex[i] / den

    def forget(z, lbd):
        f = lbd + (1.0 - lbd) * _sigmoid(z)
        return jnp.log(jnp.maximum(f, MIN_FORGET)), (1.0 - lbd) * _sigmoid(-z)

    q_ref[...] = _silu(proj(0))
    gf, kf = forget(proj(1), lb[0:1])
    gf_ref[...] = gf
    kf_ref[...] = kf
    gb, kb = forget(proj(2), lb[1:2])
    gb_ref[...] = gb
    kb_ref[...] = kb
    v_ref[...] = proj(3)
    og_ref[...] = _silu(proj(4))


def _hg_proj_call(layer, h, w_hg, lb_raw):
    s, d = h.shape
    t = min(TOKEN_BLOCK, s)
    row = lambda i: (i, 0)
    out = jax.ShapeDtypeStruct((s, HG_WIDTH), F32)
    return pl.pallas_call(
        functools.partial(_hg_proj_kernel, layer),
        grid=(s // t,),
        in_specs=[pl.BlockSpec((t, d), row), _full(w_hg.shape), _full(lb_raw.shape)],
        out_specs=[pl.BlockSpec((t, HG_WIDTH), row)] * 7,
        out_shape=[out] * 7,
        compiler_params=_cparams("parallel"),
        name="hg_proj",
    )(h, w_hg, lb_raw)


def _gate_kernel(h_ref, w_ref, g_ref):
    g_ref[...] = _sigmoid(jnp.dot(h_ref[...], w_ref[...], preferred_element_type=F32)).astype(BF16)


def _gate_call(h, w_gate):
    s, d = h.shape
    t = min(TOKEN_BLOCK, s)
    n = w_gate.shape[1]
    return pl.pallas_call(
        _gate_kernel,
        grid=(s // t,),
        in_specs=[pl.BlockSpec((t, d), lambda i: (i, 0)), _full(w_gate.shape)],
        out_specs=pl.BlockSpec((t, n), lambda i: (i, 0)),
        out_shape=jax.ShapeDtypeStruct((s, n), BF16),
        compiler_params=_cparams("parallel"),
        name="gate_proj",
    )(h, w_gate)


def _attn_kernel(q_ref, k_ref, vt_ref, o_ref, m_ref, acc_ref):
    g, tq, hd = q_ref.shape
    s = k_ref.shape[1]
    tk = min(ATT_KV_BLOCK, s)
    heads_per_part = g // ATT_ROW_PARTS
    cols = heads_per_part * tq
    acc_ref[...] = jnp.zeros(acc_ref.shape, F32)

    def scores(ks, r):
        q = q_ref[r * heads_per_part:(r + 1) * heads_per_part].reshape(cols, hd)
        return lax.dot_general(ks, q, (((1,), (1,)), ((), ())), preferred_element_type=F32)

    ks0 = k_ref[0, pl.ds(0, min(ATT_INIT_KEYS, tk)), :]
    for r in range(ATT_ROW_PARTS):
        m_ref[:, pl.ds(r * cols, cols)] = jnp.max(scores(ks0, r), axis=0, keepdims=True)

    def body(j, carry):
        start = pl.multiple_of(j * tk, tk)
        ks = k_ref[0, pl.ds(start, tk), :]
        vt = vt_ref[0, :, pl.ds(start, tk)]
        parts = [pl.ds(r * cols, cols) for r in range(ATT_ROW_PARTS)]
        sts = [scores(ks, r) for r in range(ATT_ROW_PARTS)]
        pvs, m_news, lag = [], [], []
        for sl, st in zip(parts, sts):
            m_old = m_ref[:, sl]
            m_blk = jnp.max(st, axis=0, keepdims=True)
            p = jnp.exp2(st - m_old).astype(BF16)
            pvs.append(jnp.dot(vt, p, preferred_element_type=F32))
            m_news.append(jnp.maximum(m_old, m_blk))
            lag.append(jnp.max(m_blk - m_old))
        worst_lag = functools.reduce(jnp.maximum, lag)
        safe = worst_lag <= ATT_MAX_LAG

        @pl.when(safe)
        def _():
            for sl, pv, m_new in zip(parts, pvs, m_news):
                acc_ref[:, sl] = (acc_ref[:, sl] + pv) * jnp.exp2(m_ref[:, sl] - m_new)
                m_ref[:, sl] = m_new

        @pl.when(jnp.logical_not(safe))
        def _():
            for r, sl in enumerate(parts):
                st = scores(ks, r)
                m_old = m_ref[:, sl]
                m_new = jnp.maximum(m_old, jnp.max(st, axis=0, keepdims=True))
                p = jnp.exp2(st - m_new).astype(BF16)
                acc_ref[:, sl] = (jnp.exp2(m_old - m_new) * acc_ref[:, sl]
                                  + jnp.dot(vt, p, preferred_element_type=F32))
                m_ref[:, sl] = m_new
        return carry

    lax.fori_loop(0, s // tk, body, 0)
    acc = acc_ref[...]
    o_t = acc[:hd] / acc[hd:hd + 1]
    for i in range(g):
        o_ref[:, i * hd:(i + 1) * hd] = o_t[:, i * tq:(i + 1) * tq].T.astype(BF16)


def _attn_call(q, k, vt):
    _, s, hd = q.shape
    vr = vt.shape[1]
    tq = min(ATT_Q_BLOCK, s)
    return pl.pallas_call(
        _attn_kernel,
        grid=(ATT_KV_HEADS, s // tq),
        in_specs=[
            pl.BlockSpec((ATT_GROUP, tq, hd), lambda h, i: (h, i, 0)),
            pl.BlockSpec((1, s, hd), lambda h, i: (h, 0, 0)),
            pl.BlockSpec((1, vr, s), lambda h, i: (h, 0, 0)),
        ],
        out_specs=pl.BlockSpec((tq, ATT_GROUP * hd), lambda h, i: (i, h)),
        out_shape=jax.ShapeDtypeStruct((s, ATT_HEADS * hd), BF16),
        scratch_shapes=[pltpu.VMEM((1, ATT_GROUP * tq), F32),
                        pltpu.VMEM((vr, ATT_GROUP * tq), F32)],
        compiler_params=_cparams("parallel", "parallel"),
        name="attention",
    )(q, k, vt)


def _chunk_cumsum(g, reverse):
    c = g.shape[0]
    row = lax.broadcasted_iota(I32, g.shape, 0)
    x = g
    step = 1
    while step < c:
        if not reverse:
            x = x + jnp.where(row >= step, pltpu.roll(x, step, 0), 0.0)
        else:
            x = x + jnp.where(row < c - step, pltpu.roll(x, c - step, 0), 0.0)
        step *= 2
    return x


def _boundary_rows(b_ref, half, idx):
    c, dd = b_ref.shape
    blk = 2 * half
    pieces = []
    if blk >= 8:
        for s0 in range(0, c, blk):
            pieces.append(jnp.broadcast_to(b_ref[pl.ds(s0 + idx, 1), :], (blk, dd)))
    else:
        sub = lax.broadcasted_iota(I32, (8, dd), 0)
        for v0 in range(0, c, 8):
            piece = jnp.broadcast_to(b_ref[pl.ds(v0 + idx, 1), :], (8, dd))
            for s0 in range(v0 + blk, v0 + 8, blk):
                row = jnp.broadcast_to(b_ref[pl.ds(s0 + idx, 1), :], (8, dd))
                piece = jnp.where(sub >= s0 - v0, row, piece)
            pieces.append(piece)
    return jnp.concatenate(pieces, axis=0)


def _hg_scan_body(reverse, q_ref, k_ref, v_ref, g_ref, mask_ref, st_ref, b_ref):
    c = q_ref.shape[0]
    heads = [slice(h * HG_DIM, (h + 1) * HG_DIM) for h in range(q_ref.shape[1] // HG_DIM)]
    nt = (((1,), (1,)), ((), ()))
    q = q_ref[...]
    k = k_ref[...]
    vb = v_ref[...].astype(BF16)
    b = _chunk_cumsum(g_ref[...] * LOG2_E, reverse)
    b_ref[...] = b
    b_edge = b_ref[pl.ds(c - 1 if not reverse else 0, 1), :]
    qh = (q * jnp.exp2(b)).astype(BF16)
    kh = (k * jnp.exp2(b_edge - b)).astype(BF16)
    st_decay = jnp.exp2(b_edge)
    outs = []
    for h, sl in enumerate(heads):
        st = st_ref[h]
        outs.append(lax.dot_general(qh[:, sl], st.astype(BF16), nt, preferred_element_type=F32))
        st_ref[h] = st * st_decay[:, sl] + lax.dot_general(
            vb[:, sl], kh[:, sl], (((0,), (0,)), ((), ())), preferred_element_type=F32)

    qb = q.astype(BF16)
    kb = k.astype(BF16)
    scores = [mask_ref[0] * lax.dot_general(qb[:, sl], kb[:, sl], nt, preferred_element_type=F32)
              for sl in heads]
    half = c // 2
    level = 1
    while half >= 1:
        d = b - _boundary_rows(b_ref, half, half - 1 if not reverse else half)
        t = jnp.exp2(-jnp.abs(d))
        qt = (q * t).astype(BF16)
        kt = (k * t).astype(BF16)
        mask = mask_ref[level]
        scores = [a + mask * lax.dot_general(qt[:, sl], kt[:, sl], nt, preferred_element_type=F32)
                  for a, sl in zip(scores, heads)]
        half //= 2
        level += 1
    return [o + jnp.dot(a.astype(BF16), vb[:, sl], preferred_element_type=F32)
            for o, a, sl in zip(outs, scores, heads)]


def _hg_fwd_kernel(q_ref, k_ref, v_ref, g_ref, mask_ref, o_ref, st_ref, b_ref):
    @pl.when(pl.program_id(0) == 0)
    def _():
        st_ref[...] = jnp.zeros_like(st_ref)

    outs = _hg_scan_body(False, q_ref, k_ref, v_ref, g_ref, mask_ref, st_ref, b_ref)
    o_ref[...] = jnp.concatenate(outs, axis=1)


def _hg_bwd_kernel(q_ref, k_ref, v_ref, g_ref, mask_ref, of_ref, og_ref, nw_ref, o_ref, st_ref,
                   b_ref):
    @pl.when(pl.program_id(0) == 0)
    def _():
        st_ref[...] = jnp.zeros_like(st_ref)

    outs = _hg_scan_body(True, q_ref, k_ref, v_ref, g_ref, mask_ref, st_ref, b_ref)
    normed = []
    for h, o_b in enumerate(outs):
        o = of_ref[:, h * HG_DIM:(h + 1) * HG_DIM] + o_b
        ms = jnp.mean(o * o, axis=-1, keepdims=True)
        normed.append(o * lax.rsqrt(ms + NORM_EPS) * nw_ref[...])
    o_ref[...] = (jnp.concatenate(normed, axis=1) * og_ref[...]).astype(BF16)


def _level_masks(c, reverse):
    ii = np.arange(c)[:, None]
    jj = np.arange(c)[None, :]
    masks = [ii == jj]
    half = c // 2
    while half >= 1:
        blk = 2 * half
        same = (ii // blk) == (jj // blk)
        q_hi, k_hi = (ii % blk) >= half, (jj % blk) >= half
        masks.append(same & (q_hi & ~k_hi if not reverse else ~q_hi & k_hi))
        half //= 2
    return jnp.asarray(np.stack(masks).astype(np.float32))


def _hg_call(q, kf, gf, kb, gb, v, og, nw):
    s = q.shape[0]
    c = min(HG_CHUNK, s)
    n = s // c
    mask_f = _level_masks(c, False)
    mask_b = _level_masks(c, True)
    fwd = lambda i: (i, 0)
    bwd = lambda i: (n - 1 - i, 0)
    blk = lambda m: pl.BlockSpec((c, HG_WIDTH), m)
    scratch = [pltpu.VMEM((HG_HEADS, HG_DIM, HG_DIM), F32), pltpu.VMEM((c, HG_WIDTH), F32)]
    o_fwd = pl.pallas_call(
        _hg_fwd_kernel,
        grid=(n,),
        in_specs=[blk(fwd)] * 4 + [_full(mask_f.shape)],
        out_specs=blk(fwd),
        out_shape=jax.ShapeDtypeStruct((s, HG_WIDTH), F32),
        scratch_shapes=scratch,
        compiler_params=_cparams("arbitrary"),
        name="hg_scan_fwd",
    )(q, kf, v, gf, mask_f)
    return pl.pallas_call(
        _hg_bwd_kernel,
        grid=(n,),
        in_specs=[blk(bwd)] * 4 + [_full(mask_b.shape), blk(bwd), blk(bwd), _full((1, HG_DIM))],
        out_specs=blk(bwd),
        out_shape=jax.ShapeDtypeStruct((s, HG_WIDTH), BF16),
        scratch_shapes=scratch,
        compiler_params=_cparams("arbitrary"),
        name="hg_scan_bwd",
    )(q, kb, v, gb, mask_b, o_fwd, og, nw)


def _merge_kernel(x_ref, oa_ref, oh_ref, g_ref, wa_ref, wh_ref, wo_ref, g1_ref, sh_ref, sc_ref,
                  nw_ref, wr_ref, xo_ref, h_ref, aff_ref):
    d = x_ref.shape[1]
    ya = jnp.dot(oa_ref[...], wa_ref[...], preferred_element_type=F32)
    yh = jnp.dot(oh_ref[...], wh_ref[...], preferred_element_type=F32)
    merged = g_ref[:, :d].astype(F32) * ya + g_ref[:, d:].astype(F32) * yh
    y = jnp.dot(merged.astype(BF16), wo_ref[...], preferred_element_type=F32)
    xn = x_ref[...] + g1_ref[...] * y
    xo_ref[...] = xn
    h = _rms_mod(xn, nw_ref[...], sc_ref[...], sh_ref[...])
    h_ref[...] = h.astype(BF16)
    logits = lax.dot_general(wr_ref[...], h, (((1,), (1,)), ((), ())), precision=HIGHEST,
                             preferred_element_type=F32)
    ex = jnp.exp(logits - jnp.max(logits, axis=0, keepdims=True))
    aff_ref[...] = ex / jnp.sum(ex, axis=0, keepdims=True)


def _merge_call(x, o_att, o_hg, g, w_ba, w_bh, w_out, gate1, shift2, scale2, nw2, w_router_t):
    s, d = x.shape
    t = min(TOKEN_BLOCK, s)
    row = lambda i: (i, 0)
    return pl.pallas_call(
        _merge_kernel,
        grid=(s // t,),
        in_specs=[
            pl.BlockSpec((t, d), row), pl.BlockSpec((t, ATT_WIDTH), row),
            pl.BlockSpec((t, HG_WIDTH), row), pl.BlockSpec((t, 2 * d), row),
            _full(w_ba.shape), _full(w_bh.shape), _full(w_out.shape),
            _full((1, d)), _full((1, d)), _full((1, d)), _full((1, d)), _full(w_router_t.shape),
        ],
        out_specs=[pl.BlockSpec((t, d), row), pl.BlockSpec((t, d), row),
                   pl.BlockSpec((N_EXPERTS, t), lambda i: (0, i))],
        out_shape=[jax.ShapeDtypeStruct((s, d), F32), jax.ShapeDtypeStruct((s, d), BF16),
                   jax.ShapeDtypeStruct((N_EXPERTS, s), F32)],
        compiler_params=_cparams("parallel"),
        name="merge_router",
    )(x, o_att, o_hg, g, w_ba, w_bh, w_out, gate1, shift2, scale2, nw2, w_router_t)


def _select_kernel(cap, aff_ref, tri_ref, pos_ref, starts_ref):
    n_exp, s = aff_ref.shape
    t = tri_ref.shape[0]
    n_blk = s // t
    aff = aff_ref[...]

    def bit_step(i, thr_bits):
        cand = thr_bits | lax.shift_left(jnp.int32(1), jnp.int32(30) - i)
        cnt = jnp.sum((aff >= pltpu.bitcast(cand, F32)).astype(F32), axis=1, keepdims=True)
        return jnp.where(cnt >= cap, cand, thr_bits)

    thr = pltpu.bitcast(lax.fori_loop(0, 31, bit_step, jnp.zeros((n_exp, 1), I32)), F32)
    n_gt = jnp.sum((aff > thr).astype(F32), axis=1, keepdims=True)
    need_eq = cap - n_gt
    tri = tri_ref[...]
    blk_lane = lax.broadcasted_iota(I32, starts_ref.shape, 1)

    starts_ref[...] = jnp.zeros(starts_ref.shape, I32)

    def blk_step(j, carry):
        c_eq, c_sel = carry
        start = pl.multiple_of(j * t, t)
        bt = aff_ref[:, pl.ds(start, t)]
        eq = bt == thr
        eq_f = eq.astype(BF16)
        eq_excl = c_eq + jnp.dot(eq_f, tri, preferred_element_type=F32) - eq_f.astype(F32)
        sel = (bt > thr) | (eq & (eq_excl < need_eq))
        sel_f = sel.astype(BF16)
        sel_incl = c_sel + jnp.dot(sel_f, tri, preferred_element_type=F32)
        pos_ref[:, pl.ds(start, t)] = jnp.where(sel, sel_incl - 1.0, -1.0).astype(I32)
        starts_ref[...] = jnp.where(blk_lane == j, c_sel.astype(I32), starts_ref[...])
        return (c_eq + jnp.sum(eq_f.astype(F32), axis=1, keepdims=True),
                c_sel + jnp.sum(sel_f.astype(F32), axis=1, keepdims=True))

    zero = jnp.zeros((n_exp, 1), F32)
    _, c_sel = lax.fori_loop(0, n_blk, blk_step, (zero, zero))
    starts_ref[...] = jnp.where(blk_lane >= n_blk, c_sel.astype(I32), starts_ref[...])


def _select_call(aff_t, cap):
    n_exp, s = aff_t.shape
    t = min(ROUTE_BLOCK, s)
    ii = np.arange(t)
    tri = jnp.asarray((ii[:, None] <= ii[None, :]).astype(np.float32), dtype=BF16)
    return pl.pallas_call(
        functools.partial(_select_kernel, float(cap)),
        grid=(1,),
        in_specs=[_full((n_exp, s)), _full((t, t))],
        out_specs=[_full((n_exp, s)), _full((n_exp, LANES))],
        out_shape=[jax.ShapeDtypeStruct((n_exp, s), I32), jax.ShapeDtypeStruct((n_exp, LANES), I32)],
        compiler_params=_cparams("arbitrary"),
        name="route_select",
    )(aff_t, tri)


def _gather_kernel(cap, per, starts_ref, pos_ref, h_ref, xe_ref, acc_ref):
    e = pl.program_id(0)
    b = pl.program_id(1)
    t = h_ref.shape[0]

    @pl.when(b == 0)
    def _():
        acc_ref[...] = jnp.zeros_like(acc_ref)

    lo = starts_ref[e, b * per]
    hi = starts_ref[e, (b + 1) * per]
    c0 = lax.shift_right_logical(lo, ROW_CHUNK_LOG2)
    pos = pos_ref[pl.ds(e, 1), :]
    slot = lax.broadcasted_iota(I32, (ROW_CHUNK, t), 0)
    for kk in range(t // ROW_CHUNK + 1):
        c = c0 + kk

        @pl.when((hi > lo) & (c * ROW_CHUNK < hi) & (c < cap // ROW_CHUNK))
        def _():
            base = pl.multiple_of(c * ROW_CHUNK, ROW_CHUNK)
            onehot = (pos - base == slot).astype(BF16)
            acc_ref[pl.ds(base, ROW_CHUNK), :] += jnp.dot(onehot, h_ref[...],
                                                          preferred_element_type=F32)

    @pl.when(b == pl.num_programs(1) - 1)
    def _():
        xe_ref[0] = acc_ref[...].astype(BF16)


def _gather_call(starts, pos_t, h, cap):
    s, d = h.shape
    t = min(GATHER_BLOCK, s)
    return pl.pallas_call(
        functools.partial(_gather_kernel, cap, t // min(ROUTE_BLOCK, s)),
        grid_spec=pltpu.PrefetchScalarGridSpec(
            num_scalar_prefetch=1,
            grid=(N_EXPERTS, s // t),
            in_specs=[pl.BlockSpec((N_EXPERTS, t), lambda e, b, st: (0, b)),
                      pl.BlockSpec((t, d), lambda e, b, st: (b, 0))],
            out_specs=pl.BlockSpec((1, cap, d), lambda e, b, st: (e, 0, 0)),
            scratch_shapes=[pltpu.VMEM((cap, d), F32)],
        ),
        out_shape=jax.ShapeDtypeStruct((N_EXPERTS, cap, d), BF16),
        compiler_params=_cparams("parallel", "arbitrary"),
        name="expert_gather",
    )(starts, pos_t, h)


def _ffn_kernel(xe_ref, wg_ref, wu_ref, wd_ref, ye_ref, acc_ref):
    f = pl.program_id(1)
    last = pl.num_programs(1) - 1
    rows = xe_ref.shape[1] // FF_ROW_PARTS
    wg = wg_ref[0, 0].astype(BF16)
    wu = wu_ref[0, 0].astype(BF16)
    wd = wd_ref[0, 0].astype(BF16)
    parts = [pl.ds(r * rows, rows) for r in range(FF_ROW_PARTS)]
    gate_up = []
    for sl in parts:
        x = xe_ref[0, sl, :]
        gate_up.append((jnp.dot(x, wg, preferred_element_type=F32),
                        jnp.dot(x, wu, preferred_element_type=F32)))
    hids = [(_silu(a) * u).astype(BF16) for a, u in gate_up]

    def down(hid):
        return jnp.dot(hid, wd, preferred_element_type=F32)

    @pl.when(f == 0)
    def _():
        for sl, hid in zip(parts, hids):
            acc_ref[sl, :] = down(hid)

    @pl.when((f > 0) & (f < last))
    def _():
        for sl, hid in zip(parts, hids):
            acc_ref[sl, :] += down(hid)

    @pl.when(f == last)
    def _():
        for sl, hid in zip(parts, hids):
            ye_ref[0, sl, :] = (acc_ref[sl, :] + down(hid)).astype(BF16)


def _ffn_call(layer, xe, w_gate, w_up, w_down):
    n_exp, cap, d = xe.shape
    ff = w_gate.shape[3]
    fb = min(FF_BLOCK, ff)
    assert ff // fb >= 2, "the first and last hidden blocks take different accumulate paths"
    return pl.pallas_call(
        _ffn_kernel,
        grid=(n_exp, ff // fb),
        in_specs=[pl.BlockSpec((1, cap, d), lambda e, f: (e, 0, 0)),
                  pl.BlockSpec((1, 1, d, fb), lambda e, f: (layer, e, 0, f)),
                  pl.BlockSpec((1, 1, d, fb), lambda e, f: (layer, e, 0, f)),
                  pl.BlockSpec((1, 1, fb, d), lambda e, f: (layer, e, f, 0))],
        out_specs=pl.BlockSpec((1, cap, d), lambda e, f: (e, 0, 0)),
        out_shape=jax.ShapeDtypeStruct((n_exp, cap, d), BF16),
        scratch_shapes=[pltpu.VMEM((cap, d), F32)],
        compiler_params=_cparams("parallel", "arbitrary"),
        name="expert_ffn",
    )(xe, w_gate, w_up, w_down)


def _combine_kernel(n_chunks, n_cand, starts_ref, pos_ref, w_ref, x_ref, g2_ref, ye_ref, xo_ref,
                    acc_ref, buf_ref, sem_ref):
    b = pl.program_id(0)
    t = x_ref.shape[0]
    n_exp = pos_ref.shape[1]

    def span(e):
        lo = starts_ref[e, b]
        return lo, starts_ref[e, b + 1], lax.shift_right_logical(lo, ROW_CHUNK_LOG2)

    def needed(sp, kk):
        lo, hi, c0 = sp
        c = c0 + kk
        return (hi > lo) & (lax.shift_left(c, ROW_CHUNK_LOG2) < hi) & (c < n_chunks)

    def chunk_copy(e, sp, kk, slot):
        c = jnp.minimum(sp[2] + kk, n_chunks - 1)
        rows = pl.ds(pl.multiple_of(lax.shift_left(c, ROW_CHUNK_LOG2), ROW_CHUNK), ROW_CHUNK)
        return pltpu.make_async_copy(ye_ref.at[e, rows, :], buf_ref.at[slot, kk],
                                     sem_ref.at[slot, kk])

    def start_expert(e, slot):
        sp = span(e)
        for kk in range(n_cand):
            @pl.when(needed(sp, kk))
            def _():
                chunk_copy(e, sp, kk, slot).start()

    def wait_expert(e, sp, slot):
        for kk in range(n_cand):
            @pl.when(needed(sp, kk))
            def _():
                chunk_copy(e, sp, kk, slot).wait()

    acc_ref[...] = jnp.zeros_like(acc_ref)
    start_expert(0, 0)
    lane = lax.broadcasted_iota(I32, pos_ref.shape, 1)
    row_slot = lax.broadcasted_iota(I32, (t, ROW_CHUNK), 1)

    def expert_step(e, carry):
        slot = jnp.bitwise_and(e, 1)

        @pl.when(e + 1 < n_exp)
        def _():
            start_expert(e + 1, 1 - slot)

        sp = span(e)
        wait_expert(e, sp, slot)
        pos = jnp.sum(jnp.where(lane == e, pos_ref[...], 0), axis=1, keepdims=True)
        wgt = jnp.sum(jnp.where(lane == e, w_ref[...], 0.0), axis=1, keepdims=True)
        for kk in range(n_cand):
            @pl.when(needed(sp, kk))
            def _():
                base = lax.shift_left(sp[2] + kk, ROW_CHUNK_LOG2)
                onehot = (pos - base == row_slot).astype(BF16)
                acc_ref[...] += wgt * jnp.dot(onehot, buf_ref[slot, kk],
                                              preferred_element_type=F32)
        return carry

    lax.fori_loop(0, n_exp, expert_step, 0)
    xo_ref[...] = x_ref[...] + g2_ref[...] * acc_ref[...]


def _combine_call(starts, pos_tok, w_tok, x, gate2, ye):
    s, d = x.shape
    n_exp, cap, _ = ye.shape
    t = min(ROUTE_BLOCK, s)
    n_chunks = cap // ROW_CHUNK
    n_cand = min(t // ROW_CHUNK + 1, n_chunks)
    return pl.pallas_call(
        functools.partial(_combine_kernel, n_chunks, n_cand),
        grid_spec=pltpu.PrefetchScalarGridSpec(
            num_scalar_prefetch=1,
            grid=(s // t,),
            in_specs=[pl.BlockSpec((t, n_exp), lambda b, st: (b, 0)),
                      pl.BlockSpec((t, n_exp), lambda b, st: (b, 0)),
                      pl.BlockSpec((t, d), lambda b, st: (b, 0)),
                      pl.BlockSpec((1, d), lambda b, st: (0, 0)),
                      pl.BlockSpec(memory_space=pl.ANY)],
            out_specs=pl.BlockSpec((t, d), lambda b, st: (b, 0)),
            scratch_shapes=[pltpu.VMEM((t, d), F32),
                            pltpu.VMEM((2, n_cand, ROW_CHUNK, d), BF16),
                            pltpu.SemaphoreType.DMA((2, n_cand))],
        ),
        out_shape=jax.ShapeDtypeStruct((s, d), F32),
        compiler_params=_cparams("arbitrary"),
        name="expert_combine",
    )(starts, pos_tok, w_tok, x, gate2, ye)


def _rope_tables(seq_len):
    rows = seq_len // GRID_W
    row_id = jnp.repeat(jnp.arange(rows, dtype=F32), GRID_W)
    col_id = jnp.tile(jnp.arange(GRID_W, dtype=F32), rows)
    inv_freq = ROPE_THETA ** (-jnp.arange(0, ROPE_AXIS_DIM, 2, dtype=F32) / ROPE_AXIS_DIM)
    ang = jnp.stack([row_id[:, None] * inv_freq, col_id[:, None] * inv_freq], axis=1)
    cos, sin = jnp.cos(ang), jnp.sin(ang)
    cos_h = jnp.concatenate([cos, cos], axis=-1).reshape(seq_len, ATT_HEAD_DIM)
    sin_h = jnp.concatenate([-sin, sin], axis=-1).reshape(seq_len, ATT_HEAD_DIM)
    reps = LANES // ATT_HEAD_DIM
    return jnp.tile(cos_h, (1, reps)), jnp.tile(sin_h, (1, reps))


def _head_mean_operator():
    ii = np.arange(ATT_WIDTH)
    same = (ii[:, None] // ATT_HEAD_DIM) == (ii[None, :] // ATT_HEAD_DIM)
    return jnp.asarray(same.astype(np.float32) / ATT_HEAD_DIM)


def kernel(x, c, ada_w, ada_b, norm_mix_w, norm_ffn_w, w_in, q_norm_w, k_norm_w, hg_lower_bounds,
           hg_norm_w, w_branch_att, w_branch_hg, w_out, w_router, w_exp_gate, w_exp_up, w_exp_down):
    batch, s, d = x.shape
    assert batch == 1, "kernel is written for a single sequence"
    n_layers = ada_w.shape[0]
    cap = EC_CAPACITY_FACTOR * s // N_EXPERTS
    assert cap % ROW_CHUNK == 0 and s % min(ROUTE_BLOCK, s) == 0

    cos, sin = _rope_tables(s)
    bd = _head_mean_operator()
    mods = _ada_mod(c, ada_w, ada_b)
    qkv_w = ATT_WIDTH + 2 * ATT_KV_WIDTH
    hg_w = 5 * HG_WIDTH
    xs = x[0]
    for l in range(n_layers):
        shift1, scale1, gate1, shift2, scale2, gate2 = [
            mods[l, i * d:(i + 1) * d].reshape(1, d) for i in range(6)]
        qk_w = ATT_WIDTH + ATT_KV_WIDTH
        h, q, k, v = _qkv_call(
            xs, shift1, scale1, norm_mix_w[l].reshape(1, d), w_in[l, :, :qk_w].astype(BF16),
            w_in[l, :, qk_w:qkv_w].T.astype(BF16),
            jnp.tile(q_norm_w[l], ATT_HEADS).reshape(1, ATT_WIDTH),
            jnp.tile(k_norm_w[l], ATT_KV_HEADS).reshape(1, ATT_KV_WIDTH), cos, sin, bd)
        hq, kf, gf, kb, gb, hv, og = _hg_proj_call(
            l, h, w_in[l, :, qkv_w:qkv_w + hg_w].astype(BF16), hg_lower_bounds)
        g = _gate_call(h, w_in[l, :, qkv_w + hg_w:].astype(BF16))
        o_att = _attn_call(q, k, v)
        o_hg = _hg_call(hq, kf, gf, kb, gb, hv, og, hg_norm_w[l].reshape(1, HG_DIM))
        xs, h2, aff_t = _merge_call(
            xs, o_att, o_hg, g, w_branch_att[l].astype(BF16), w_branch_hg[l].astype(BF16),
            w_out[l].astype(BF16), gate1, shift2, scale2, norm_ffn_w[l].reshape(1, d),
            w_router[l].T)
        pos_t, starts = _select_call(aff_t, cap)
        xe = _gather_call(starts, pos_t, h2, cap)
        ye = _ffn_call(l, xe, w_exp_gate, w_exp_up, w_exp_down)
        xs = _combine_call(starts, pos_t.T, aff_t.T, xs, gate2, ye)
    return xs[None]
```

```python
import functools

import numpy as np
import jax
import jax.numpy as jnp
from jax import lax
from jax.experimental import pallas as pl
from jax.experimental.pallas import tpu as pltpu

F32 = jnp.float32
BF16 = jnp.bfloat16
I32 = jnp.int32
HIGHEST = lax.Precision.HIGHEST

GRID_W = 64
ATT_HEADS = 8
ATT_KV_HEADS = 2
ATT_GROUP = ATT_HEADS // ATT_KV_HEADS
ATT_HEAD_DIM = 64
ATT_WIDTH = ATT_HEADS * ATT_HEAD_DIM
ATT_KV_WIDTH = ATT_KV_HEADS * ATT_HEAD_DIM
ROPE_AXIS_DIM = ATT_HEAD_DIM // 2
ROPE_THETA = 10000.0
HG_HEADS = 4
HG_DIM = 128
HG_WIDTH = HG_HEADS * HG_DIM
MIN_FORGET = 1e-6
N_EXPERTS = 16
EC_CAPACITY_FACTOR = 2
NORM_EPS = 1e-6
LOG2_E = 1.4426950408889634

LANES = 128
VMEM_LIMIT_BYTES = 56 * 1024 * 1024

TOKEN_BLOCK = 512
ATT_Q_BLOCK = 512
ATT_KV_BLOCK = 2048
ATT_ROW_PARTS = 4
ATT_ONES_ROWS = 16
ATT_MAX_LAG = 64.0
ATT_INIT_KEYS = 256
HG_CHUNK = 128
HG_SUB = 16
ROUTE_BLOCK = 512
GATHER_BLOCK = 1024
ROW_CHUNK_LOG2 = 8
ROW_CHUNK = 1 << ROW_CHUNK_LOG2
FF_BLOCK = 512
FF_ROW_PARTS = 2


def _cparams(*sem):
    return pltpu.CompilerParams(dimension_semantics=sem, vmem_limit_bytes=VMEM_LIMIT_BYTES)


def _sigmoid(x):
    return 1.0 / (1.0 + jnp.exp(-x))


def _silu(x):
    return x * _sigmoid(x)


def _full(shape):
    return pl.BlockSpec(shape, lambda *_: (0,) * len(shape))


def _ada_kernel(c_ref, w_ref, b_ref, o_ref):
    ca = _silu(c_ref[...])
    o_ref[0] = jnp.dot(ca, w_ref[0], precision=HIGHEST, preferred_element_type=F32) + b_ref[0]


def _ada_mod(c, ada_w, ada_b):
    n_layers, d, w6 = ada_w.shape
    col = 1536
    c8 = jnp.broadcast_to(c, (8, d))
    out = pl.pallas_call(
        _ada_kernel,
        grid=(n_layers, w6 // col),
        in_specs=[
            pl.BlockSpec((8, d), lambda l, j: (0, 0)),
            pl.BlockSpec((1, d, col), lambda l, j: (l, 0, j)),
            pl.BlockSpec((1, 1, col), lambda l, j: (l, 0, j)),
        ],
        out_specs=pl.BlockSpec((1, 8, col), lambda l, j: (l, 0, j)),
        out_shape=jax.ShapeDtypeStruct((n_layers, 8, w6), F32),
        compiler_params=_cparams("parallel", "parallel"),
        name="ada_mod",
    )(c8, ada_w, ada_b.reshape(n_layers, 1, w6))
    return out[:, 0, :]


def _rms_mod(x, nw, scale, shift):
    ms = jnp.mean(x * x, axis=-1, keepdims=True)
    return (x * lax.rsqrt(ms + NORM_EPS) * nw) * (1.0 + scale) + shift


def _qkv_kernel(x_ref, sh_ref, sc_ref, nw_ref, w_ref, wvt_ref, qnw_ref, knw_ref, cos_ref,
                sin_ref, bd_ref, h_ref, q_ref, k_ref, vt_ref):
    h = _rms_mod(x_ref[...], nw_ref[...], sc_ref[...], sh_ref[...])
    hb = h.astype(BF16)
    h_ref[...] = hb
    proj = jnp.dot(hb, w_ref[...], preferred_element_type=F32)
    aq = proj[:, :ATT_WIDTH]
    ak = proj[:, ATT_WIDTH:]
    avt = lax.dot_general(wvt_ref[...], hb, (((1,), (1,)), ((), ())),
                          preferred_element_type=F32)
    bd = bd_ref[...]

    def head_mean(sq, op):
        hi = sq.astype(BF16)
        lo = (sq - hi.astype(F32)).astype(BF16)
        return (jnp.dot(hi, op, preferred_element_type=F32)
                + jnp.dot(lo, op, preferred_element_type=F32))

    qms = head_mean(aq * aq, bd)
    kms = head_mean(ak * ak, bd[:ATT_KV_WIDTH, :ATT_KV_WIDTH])
    qn = aq * lax.rsqrt(qms + NORM_EPS) * qnw_ref[...]
    kn = ak * lax.rsqrt(kms + NORM_EPS) * knw_ref[...]
    cos = cos_ref[...]
    sin = sin_ref[...]
    lane = lax.broadcasted_iota(I32, cos.shape, 1)
    first_half = (lane % ROPE_AXIS_DIM) < (ROPE_AXIS_DIM // 2)

    def rope(xs):
        up = pltpu.roll(xs, LANES - ROPE_AXIS_DIM // 2, 1)
        dn = pltpu.roll(xs, ROPE_AXIS_DIM // 2, 1)
        return xs * cos + jnp.where(first_half, up, dn) * sin

    q_scale = ATT_HEAD_DIM ** -0.5 * LOG2_E
    for j in range(ATT_WIDTH // LANES):
        qr = rope(qn[:, j * LANES:(j + 1) * LANES]) * q_scale
        q_ref[2 * j] = qr[:, :ATT_HEAD_DIM].astype(BF16)
        q_ref[2 * j + 1] = qr[:, ATT_HEAD_DIM:].astype(BF16)
    kr = rope(kn)
    ones = jnp.ones((ATT_ONES_ROWS, avt.shape[1]), F32)
    for j in range(ATT_KV_HEADS):
        k_ref[j] = kr[:, j * ATT_HEAD_DIM:(j + 1) * ATT_HEAD_DIM].astype(BF16)
        vj = avt[j * ATT_HEAD_DIM:(j + 1) * ATT_HEAD_DIM]
        vt_ref[j] = jnp.concatenate([vj, ones], axis=0).astype(BF16)


def _qkv_call(x, shift, scale, nw, w_qk, w_v_t, qnw, knw, cos, sin, bd):
    s, d = x.shape
    t = min(TOKEN_BLOCK, s)
    vr = ATT_HEAD_DIM + ATT_ONES_ROWS
    row = lambda i: (i, 0)
    head = lambda i: (0, i, 0)
    return pl.pallas_call(
        _qkv_kernel,
        grid=(s // t,),
        in_specs=[
            pl.BlockSpec((t, d), row), _full((1, d)), _full((1, d)), _full((1, d)),
            _full(w_qk.shape), _full(w_v_t.shape), _full((1, ATT_WIDTH)), _full((1, ATT_KV_WIDTH)),
            pl.BlockSpec((t, LANES), row), pl.BlockSpec((t, LANES), row), _full(bd.shape),
        ],
        out_specs=[
            pl.BlockSpec((t, d), row),
            pl.BlockSpec((ATT_HEADS, t, ATT_HEAD_DIM), head),
            pl.BlockSpec((ATT_KV_HEADS, t, ATT_HEAD_DIM), head),
            pl.BlockSpec((ATT_KV_HEADS, vr, t), lambda i: (0, 0, i)),
        ],
        out_shape=[
            jax.ShapeDtypeStruct((s, d), BF16),
            jax.ShapeDtypeStruct((ATT_HEADS, s, ATT_HEAD_DIM), BF16),
            jax.ShapeDtypeStruct((ATT_KV_HEADS, s, ATT_HEAD_DIM), BF16),
            jax.ShapeDtypeStruct((ATT_KV_HEADS, vr, s), BF16),
        ],
        compiler_params=_cparams("parallel"),
        name="qkv_proj",
    )(x, shift, scale, nw, w_qk, w_v_t, qnw, knw, cos, sin, bd)


def _hg_proj_kernel(layer, h_ref, w_ref, lb_ref, q_ref, kf_ref, gf_ref, kb_ref, gb_ref,
                    v_ref, og_ref):
    hb = h_ref[...]

    def proj(j):
        return jnp.dot(hb, w_ref[:, j * HG_WIDTH:(j + 1) * HG_WIDTH], preferred_element_type=F32)

    lb_raw = lb_ref[...]
    n_layers = lb_raw.shape[0]
    mx = lb_raw[0]
    for i in range(1, n_layers):
        mx = jnp.maximum(mx, lb_raw[i])
    ex = [jnp.exp(lb_raw[i] - mx) for i in range(n_layers)]
    den = ex[0]
    for i in range(1, n_layers):
        den = den + ex[i]
    lb = jnp.zeros_like(mx)
    for i in range(1, layer + 1):
        lb = lb + ex[i] / den

    def forget(z, lbd):
        f = lbd + (1.0 - lbd) * _sigmoid(z)
        return jnp.log(jnp.maximum(f, MIN_FORGET)), (1.0 - lbd) * _sigmoid(-z)

    q_ref[...] = _silu(proj(0))
    gf, kf = forget(proj(1), lb[0:1])
    gf_ref[...] = gf
    kf_ref[...] = kf
    gb, kb = forget(proj(2), lb[1:2])
    gb_ref[...] = gb
    kb_ref[...] = kb
    v_ref[...] = proj(3)
    og_ref[...] = _silu(proj(4))


def _hg_proj_call(layer, h, w_hg, lb_raw):
    s, d = h.shape
    t = min(TOKEN_BLOCK, s)
    row = lambda i: (i, 0)
    out = jax.ShapeDtypeStruct((s, HG_WIDTH), F32)
    return pl.pallas_call(
        functools.partial(_hg_proj_kernel, layer),
        grid=(s // t,),
        in_specs=[pl.BlockSpec((t, d), row), _full(w_hg.shape), _full(lb_raw.shape)],
        out_specs=[pl.BlockSpec((t, HG_WIDTH), row)] * 7,
        out_shape=[out] * 7,
        compiler_params=_cparams("parallel"),
        name="hg_proj",
    )(h, w_hg, lb_raw)


def _gate_kernel(h_ref, w_ref, g_ref):
    g_ref[...] = _sigmoid(jnp.dot(h_ref[...], w_ref[...], preferred_element_type=F32)).astype(BF16)


def _gate_call(h, w_gate):
    s, d = h.shape
    t = min(TOKEN_BLOCK, s)
    n = w_gate.shape[1]
    return pl.pallas_call(
        _gate_kernel,
        grid=(s // t,),
        in_specs=[pl.BlockSpec((t, d), lambda i: (i, 0)), _full(w_gate.shape)],
        out_specs=pl.BlockSpec((t, n), lambda i: (i, 0)),
        out_shape=jax.ShapeDtypeStruct((s, n), BF16),
        compiler_params=_cparams("parallel"),
        name="gate_proj",
    )(h, w_gate)


def _attn_kernel(q_ref, k_ref, vt_ref, o_ref, m_ref, acc_ref):
    g, tq, hd = q_ref.shape
    s = k_ref.shape[1]
    tk = min(ATT_KV_BLOCK, s)
    heads_per_part = g // ATT_ROW_PARTS
    cols = heads_per_part * tq
    acc_ref[...] = jnp.zeros(acc_ref.shape, F32)

    def scores(ks, r):
        q = q_ref[r * heads_per_part:(r + 1) * heads_per_part].reshape(cols, hd)
        return lax.dot_general(ks, q, (((1,), (1,)), ((), ())), preferred_element_type=F32)

    ks0 = k_ref[0, pl.ds(0, min(ATT_INIT_KEYS, tk)), :]
    for r in range(ATT_ROW_PARTS):
        m_ref[:, pl.ds(r * cols, cols)] = jnp.max(scores(ks0, r), axis=0, keepdims=True)

    def body(j, carry):
        start = pl.multiple_of(j * tk, tk)
        ks = k_ref[0, pl.ds(start, tk), :]
        vt = vt_ref[0, :, pl.ds(start, tk)]
        parts = [pl.ds(r * cols, cols) for r in range(ATT_ROW_PARTS)]
        sts = [scores(ks, r) for r in range(ATT_ROW_PARTS)]
        pvs, m_news, lag = [], [], []
        for sl, st in zip(parts, sts):
            m_old = m_ref[:, sl]
            m_blk = jnp.max(st, axis=0, keepdims=True)
            p = jnp.exp2(st - m_old).astype(BF16)
            pvs.append(jnp.dot(vt, p, preferred_element_type=F32))
            m_news.append(jnp.maximum(m_old, m_blk))
            lag.append(jnp.max(m_blk - m_old))
        worst_lag = functools.reduce(jnp.maximum, lag)
        safe = worst_lag <= ATT_MAX_LAG

        @pl.when(safe)
        def _():
            for sl, pv, m_new in zip(parts, pvs, m_news):
                acc_ref[:, sl] = (acc_ref[:, sl] + pv) * jnp.exp2(m_ref[:, sl] - m_new)
                m_ref[:, sl] = m_new

        @pl.when(jnp.logical_not(safe))
        def _():
            for r, sl in enumerate(parts):
                st = scores(ks, r)
                m_old = m_ref[:, sl]
                m_new = jnp.maximum(m_old, jnp.max(st, axis=0, keepdims=True))
                p = jnp.exp2(st - m_new).astype(BF16)
                acc_ref[:, sl] = (jnp.exp2(m_old - m_new) * acc_ref[:, sl]
                                  + jnp.dot(vt, p, preferred_element_type=F32))
                m_ref[:, sl] = m_new
        return carry

    lax.fori_loop(0, s // tk, body, 0)
    acc = acc_ref[...]
    o_t = acc[:hd] / acc[hd:hd + 1]
    for i in range(g):
        o_ref[:, i * hd:(i + 1) * hd] = o_t[:, i * tq:(i + 1) * tq].T.astype(BF16)


def _attn_call(q, k, vt):
    _, s, hd = q.shape
    vr = vt.shape[1]
    tq = min(ATT_Q_BLOCK, s)
    return pl.pallas_call(
        _attn_kernel,
        grid=(ATT_KV_HEADS, s // tq),
        in_specs=[
            pl.BlockSpec((ATT_GROUP, tq, hd), lambda h, i: (h, i, 0)),
            pl.BlockSpec((1, s, hd), lambda h, i: (h, 0, 0)),
            pl.BlockSpec((1, vr, s), lambda h, i: (h, 0, 0)),
        ],
        out_specs=pl.BlockSpec((tq, ATT_GROUP * hd), lambda h, i: (i, h)),
        out_shape=jax.ShapeDtypeStruct((s, ATT_HEADS * hd), BF16),
        scratch_shapes=[pltpu.VMEM((1, ATT_GROUP * tq), F32),
                        pltpu.VMEM((vr, ATT_GROUP * tq), F32)],
        compiler_params=_cparams("parallel", "parallel"),
        name="attention",
    )(q, k, vt)


def _chunk_cumsum(g, reverse):
    c = g.shape[0]
    row = lax.broadcasted_iota(I32, g.shape, 0)
    x = g
    step = 1
    while step < c:
        if not reverse:
            x = x + jnp.where(row >= step, pltpu.roll(x, step, 0), 0.0)
        else:
            x = x + jnp.where(row < c - step, pltpu.roll(x, c - step, 0), 0.0)
        step *= 2
    return x


def _boundary_rows(b_ref, half, idx):
    c, dd = b_ref.shape
    blk = 2 * half
    pieces = []
    if blk >= 8:
        for s0 in range(0, c, blk):
            pieces.append(jnp.broadcast_to(b_ref[pl.ds(s0 + idx, 1), :], (blk, dd)))
    else:
        sub = lax.broadcasted_iota(I32, (8, dd), 0)
        for v0 in range(0, c, 8):
            piece = jnp.broadcast_to(b_ref[pl.ds(v0 + idx, 1), :], (8, dd))
            for s0 in range(v0 + blk, v0 + 8, blk):
                row = jnp.broadcast_to(b_ref[pl.ds(s0 + idx, 1), :], (8, dd))
                piece = jnp.where(sub >= s0 - v0, row, piece)
            pieces.append(piece)
    return jnp.concatenate(pieces, axis=0)


def _hg_scan_body(reverse, q_ref, k_ref, v_ref, g_ref, mask_ref, st_ref, b_ref):
    c = q_ref.shape[0]
    heads = [slice(h * HG_DIM, (h + 1) * HG_DIM) for h in range(q_ref.shape[1] // HG_DIM)]
    nt = (((1,), (1,)), ((), ()))
    q = q_ref[...]
    k = k_ref[...]
    vb = v_ref[...].astype(BF16)
    b = _chunk_cumsum(g_ref[...] * LOG2_E, reverse)
    b_ref[...] = b
    b_edge = b_ref[pl.ds(c - 1 if not reverse else 0, 1), :]
    qh = (q * jnp.exp2(b)).astype(BF16)
    kh = (k * jnp.exp2(b_edge - b)).astype(BF16)
    st_decay = jnp.exp2(b_edge)
    outs = []
    for h, sl in enumerate(heads):
        st = st_ref[h]
        outs.append(lax.dot_general(qh[:, sl], st.astype(BF16), nt, preferred_element_type=F32))
        st_ref[h] = st * st_decay[:, sl] + lax.dot_general(
            vb[:, sl], kh[:, sl], (((0,), (0,)), ((), ())), preferred_element_type=F32)

    qb = q.astype(BF16)
    kb = k.astype(BF16)
    scores = [mask_ref[0] * lax.dot_general(qb[:, sl], kb[:, sl], nt, preferred_element_type=F32)
              for sl in heads]
    half = c // 2
    level = 1
    while half >= 1:
        d = b - _boundary_rows(b_ref, half, half - 1 if not reverse else half)
        t = jnp.exp2(-jnp.abs(d))
        qt = (q * t).astype(BF16)
        kt = (k * t).astype(BF16)
        mask = mask_ref[level]
        scores = [a + mask * lax.dot_general(qt[:, sl], kt[:, sl], nt, preferred_element_type=F32)
                  for a, sl in zip(scores, heads)]
        half //= 2
        level += 1
    return [o + jnp.dot(a.astype(BF16), vb[:, sl], preferred_element_type=F32)
            for o, a, sl in zip(outs, scores, heads)]


def _hg_fwd_kernel(q_ref, k_ref, v_ref, g_ref, mask_ref, o_ref, st_ref, b_ref):
    @pl.when(pl.program_id(0) == 0)
    def _():
        st_ref[...] = jnp.zeros_like(st_ref)

    outs = _hg_scan_body(False, q_ref, k_ref, v_ref, g_ref, mask_ref, st_ref, b_ref)
    o_ref[...] = jnp.concatenate(outs, axis=1)


def _hg_bwd_kernel(q_ref, k_ref, v_ref, g_ref, mask_ref, of_ref, og_ref, nw_ref, o_ref, st_ref,
                   b_ref):
    @pl.when(pl.program_id(0) == 0)
    def _():
        st_ref[...] = jnp.zeros_like(st_ref)

    outs = _hg_scan_body(True, q_ref, k_ref, v_ref, g_ref, mask_ref, st_ref, b_ref)
    normed = []
    for h, o_b in enumerate(outs):
        o = of_ref[:, h * HG_DIM:(h + 1) * HG_DIM] + o_b
        ms = jnp.mean(o * o, axis=-1, keepdims=True)
        normed.append(o * lax.rsqrt(ms + NORM_EPS) * nw_ref[...])
    o_ref[...] = (jnp.concatenate(normed, axis=1) * og_ref[...]).astype(BF16)


def _level_masks(c, reverse):
    ii = np.arange(c)[:, None]
    jj = np.arange(c)[None, :]
    masks = [ii == jj]
    half = c // 2
    while half >= 1:
        blk = 2 * half
        same = (ii // blk) == (jj // blk)
        q_hi, k_hi = (ii % blk) >= half, (jj % blk) >= half
        masks.append(same & (q_hi & ~k_hi if not reverse else ~q_hi & k_hi))
        half //= 2
    return jnp.asarray(np.stack(masks).astype(np.float32))


def _hg_call(q, kf, gf, kb, gb, v, og, nw):
    s = q.shape[0]
    c = min(HG_CHUNK, s)
    n = s // c
    mask_f = _level_masks(c, False)
    mask_b = _level_masks(c, True)
    fwd = lambda i: (i, 0)
    bwd = lambda i: (n - 1 - i, 0)
    blk = lambda m: pl.BlockSpec((c, HG_WIDTH), m)
    scratch = [pltpu.VMEM((HG_HEADS, HG_DIM, HG_DIM), F32), pltpu.VMEM((c, HG_WIDTH), F32)]
    o_fwd = pl.pallas_call(
        _hg_fwd_kernel,
        grid=(n,),
        in_specs=[blk(fwd)] * 4 + [_full(mask_f.shape)],
        out_specs=blk(fwd),
        out_shape=jax.ShapeDtypeStruct((s, HG_WIDTH), F32),
        scratch_shapes=scratch,
        compiler_params=_cparams("arbitrary"),
        name="hg_scan_fwd",
    )(q, kf, v, gf, mask_f)
    return pl.pallas_call(
        _hg_bwd_kernel,
        grid=(n,),
        in_specs=[blk(bwd)] * 4 + [_full(mask_b.shape), blk(bwd), blk(bwd), _full((1, HG_DIM))],
        out_specs=blk(bwd),
        out_shape=jax.ShapeDtypeStruct((s, HG_WIDTH), BF16),
        scratch_shapes=scratch,
        compiler_params=_cparams("arbitrary"),
        name="hg_scan_bwd",
    )(q, kb, v, gb, mask_b, o_fwd, og, nw)


def _merge_kernel(x_ref, oa_ref, oh_ref, g_ref, wa_ref, wh_ref, wo_ref, g1_ref, sh_ref, sc_ref,
                  nw_ref, wr_ref, xo_ref, h_ref, aff_ref):
    d = x_ref.shape[1]
    ya = jnp.dot(oa_ref[...], wa_ref[...], preferred_element_type=F32)
    yh = jnp.dot(oh_ref[...], wh_ref[...], preferred_element_type=F32)
    merged = g_ref[:, :d].astype(F32) * ya + g_ref[:, d:].astype(F32) * yh
    y = jnp.dot(merged.astype(BF16), wo_ref[...], preferred_element_type=F32)
    xn = x_ref[...] + g1_ref[...] * y
    xo_ref[...] = xn
    h = _rms_mod(xn, nw_ref[...], sc_ref[...], sh_ref[...])
    h_ref[...] = h.astype(BF16)
    logits = lax.dot_general(wr_ref[...], h, (((1,), (1,)), ((), ())), precision=HIGHEST,
                             preferred_element_type=F32)
    ex = jnp.exp(logits - jnp.max(logits, axis=0, keepdims=True))
    aff_ref[...] = ex / jnp.sum(ex, axis=0, keepdims=True)


def _merge_call(x, o_att, o_hg, g, w_ba, w_bh, w_out, gate1, shift2, scale2, nw2, w_router_t):
    s, d = x.shape
    t = min(TOKEN_BLOCK, s)
    row = lambda i: (i, 0)
    return pl.pallas_call(
        _merge_kernel,
        grid=(s // t,),
        in_specs=[
            pl.BlockSpec((t, d), row), pl.BlockSpec((t, ATT_WIDTH), row),
            pl.BlockSpec((t, HG_WIDTH), row), pl.BlockSpec((t, 2 * d), row),
            _full(w_ba.shape), _full(w_bh.shape), _full(w_out.shape),
            _full((1, d)), _full((1, d)), _full((1, d)), _full((1, d)), _full(w_router_t.shape),
        ],
        out_specs=[pl.BlockSpec((t, d), row), pl.BlockSpec((t, d), row),
                   pl.BlockSpec((N_EXPERTS, t), lambda i: (0, i))],
        out_shape=[jax.ShapeDtypeStruct((s, d), F32), jax.ShapeDtypeStruct((s, d), BF16),
                   jax.ShapeDtypeStruct((N_EXPERTS, s), F32)],
        compiler_params=_cparams("parallel"),
        name="merge_router",
    )(x, o_att, o_hg, g, w_ba, w_bh, w_out, gate1, shift2, scale2, nw2, w_router_t)


def _select_kernel(cap, aff_ref, tri_ref, pos_ref, starts_ref):
    n_exp, s = aff_ref.shape
    t = tri_ref.shape[0]
    n_blk = s // t
    aff = aff_ref[...]

    def bit_step(i, thr_bits):
        cand = thr_bits | lax.shift_left(jnp.int32(1), jnp.int32(30) - i)
        cnt = jnp.sum((aff >= pltpu.bitcast(cand, F32)).astype(F32), axis=1, keepdims=True)
        return jnp.where(cnt >= cap, cand, thr_bits)

    thr = pltpu.bitcast(lax.fori_loop(0, 31, bit_step, jnp.zeros((n_exp, 1), I32)), F32)
    n_gt = jnp.sum((aff > thr).astype(F32), axis=1, keepdims=True)
    need_eq = cap - n_gt
    tri = tri_ref[...]
    blk_lane = lax.broadcasted_iota(I32, starts_ref.shape, 1)

    starts_ref[...] = jnp.zeros(starts_ref.shape, I32)

    def blk_step(j, carry):
        c_eq, c_sel = carry
        start = pl.multiple_of(j * t, t)
        bt = aff_ref[:, pl.ds(start, t)]
        eq = bt == thr
        eq_f = eq.astype(BF16)
        eq_excl = c_eq + jnp.dot(eq_f, tri, preferred_element_type=F32) - eq_f.astype(F32)
        sel = (bt > thr) | (eq & (eq_excl < need_eq))
        sel_f = sel.astype(BF16)
        sel_incl = c_sel + jnp.dot(sel_f, tri, preferred_element_type=F32)
        pos_ref[:, pl.ds(start, t)] = jnp.where(sel, sel_incl - 1.0, -1.0).astype(I32)
        starts_ref[...] = jnp.where(blk_lane == j, c_sel.astype(I32), starts_ref[...])
        return (c_eq + jnp.sum(eq_f.astype(F32), axis=1, keepdims=True),
                c_sel + jnp.sum(sel_f.astype(F32), axis=1, keepdims=True))

    zero = jnp.zeros((n_exp, 1), F32)
    _, c_sel = lax.fori_loop(0, n_blk, blk_step, (zero, zero))
    starts_ref[...] = jnp.where(blk_lane >= n_blk, c_sel.astype(I32), starts_ref[...])


def _select_call(aff_t, cap):
    n_exp, s = aff_t.shape
    t = min(ROUTE_BLOCK, s)
    ii = np.arange(t)
    tri = jnp.asarray((ii[:, None] <= ii[None, :]).astype(np.float32), dtype=BF16)
    return pl.pallas_call(
        functools.partial(_select_kernel, float(cap)),
        grid=(1,),
        in_specs=[_full((n_exp, s)), _full((t, t))],
        out_specs=[_full((n_exp, s)), _full((n_exp, LANES))],
        out_shape=[jax.ShapeDtypeStruct((n_exp, s), I32), jax.ShapeDtypeStruct((n_exp, LANES), I32)],
        compiler_params=_cparams("arbitrary"),
        name="route_select",
    )(aff_t, tri)


def _gather_kernel(cap, per, starts_ref, pos_ref, h_ref, xe_ref, acc_ref):
    e = pl.program_id(0)
    b = pl.program_id(1)
    t = h_ref.shape[0]

    @pl.when(b == 0)
    def _():
        acc_ref[...] = jnp.zeros_like(acc_ref)

    lo = starts_ref[e, b * per]
    hi = starts_ref[e, (b + 1) * per]
    c0 = lax.shift_right_logical(lo, ROW_CHUNK_LOG2)
    pos = pos_ref[pl.ds(e, 1), :]
    slot = lax.broadcasted_iota(I32, (ROW_CHUNK, t), 0)
    for kk in range(t // ROW_CHUNK + 1):
        c = c0 + kk

        @pl.when((hi > lo) & (c * ROW_CHUNK < hi) & (c < cap // ROW_CHUNK))
        def _():
            base = pl.multiple_of(c * ROW_CHUNK, ROW_CHUNK)
            onehot = (pos - base == slot).astype(BF16)
            acc_ref[pl.ds(base, ROW_CHUNK), :] += jnp.dot(onehot, h_ref[...],
                                                          preferred_element_type=F32)

    @pl.when(b == pl.num_programs(1) - 1)
    def _():
        xe_ref[0] = acc_ref[...].astype(BF16)


def _gather_call(starts, pos_t, h, cap):
    s, d = h.shape
    t = min(GATHER_BLOCK, s)
    return pl.pallas_call(
        functools.partial(_gather_kernel, cap, t // min(ROUTE_BLOCK, s)),
        grid_spec=pltpu.PrefetchScalarGridSpec(
            num_scalar_prefetch=1,
            grid=(N_EXPERTS, s // t),
            in_specs=[pl.BlockSpec((N_EXPERTS, t), lambda e, b, st: (0, b)),
                      pl.BlockSpec((t, d), lambda e, b, st: (b, 0))],
            out_specs=pl.BlockSpec((1, cap, d), lambda e, b, st: (e, 0, 0)),
            scratch_shapes=[pltpu.VMEM((cap, d), F32)],
        ),
        out_shape=jax.ShapeDtypeStruct((N_EXPERTS, cap, d), BF16),
        compiler_params=_cparams("parallel", "arbitrary"),
        name="expert_gather",
    )(starts, pos_t, h)


def _ffn_kernel(xe_ref, wg_ref, wu_ref, wd_ref, ye_ref, acc_ref):
    f = pl.program_id(1)
    last = pl.num_programs(1) - 1
    rows = xe_ref.shape[1] // FF_ROW_PARTS
    wg = wg_ref[0, 0].astype(BF16)
    wu = wu_ref[0, 0].astype(BF16)
    wd = wd_ref[0, 0].astype(BF16)
    parts = [pl.ds(r * rows, rows) for r in range(FF_ROW_PARTS)]
    gate_up = []
    for sl in parts:
        x = xe_ref[0, sl, :]
        gate_up.append((jnp.dot(x, wg, preferred_element_type=F32),
                        jnp.dot(x, wu, preferred_element_type=F32)))
    hids = [(_silu(a) * u).astype(BF16) for a, u in gate_up]

    def down(hid):
        return jnp.dot(hid, wd, preferred_element_type=F32)

    @pl.when(f == 0)
    def _():
        for sl, hid in zip(parts, hids):
            acc_ref[sl, :] = down(hid)

    @pl.when((f > 0) & (f < last))
    def _():
        for sl, hid in zip(parts, hids):
            acc_ref[sl, :] += down(hid)

    @pl.when(f == last)
    def _():
        for sl, hid in zip(parts, hids):
            ye_ref[0, sl, :] = (acc_ref[sl, :] + down(hid)).astype(BF16)


def _ffn_call(layer, xe, w_gate, w_up, w_down):
    n_exp, cap, d = xe.shape
    ff = w_gate.shape[3]
    fb = min(FF_BLOCK, ff)
    assert ff // fb >= 2, "the first and last hidden blocks take different accumulate paths"
    return pl.pallas_call(
        _ffn_kernel,
        grid=(n_exp, ff // fb),
        in_specs=[pl.BlockSpec((1, cap, d), lambda e, f: (e, 0, 0)),
                  pl.BlockSpec((1, 1, d, fb), lambda e, f: (layer, e, 0, f)),
                  pl.BlockSpec((1, 1, d, fb), lambda e, f: (layer, e, 0, f)),
                  pl.BlockSpec((1, 1, fb, d), lambda e, f: (layer, e, f, 0))],
        out_specs=pl.BlockSpec((1, cap, d), lambda e, f: (e, 0, 0)),
        out_shape=jax.ShapeDtypeStruct((n_exp, cap, d), BF16),
        scratch_shapes=[pltpu.VMEM((cap, d), F32)],
        compiler_params=_cparams("parallel", "arbitrary"),
        name="expert_ffn",
    )(xe, w_gate, w_up, w_down)


def _combine_kernel(n_chunks, n_cand, starts_ref, pos_ref, w_ref, x_ref, g2_ref, ye_ref, xo_ref,
                    acc_ref, buf_ref, sem_ref):
    b = pl.program_id(0)
    t = x_ref.shape[0]
    n_exp = pos_ref.shape[1]

    def span(e):
        lo = starts_ref[e, b]
        return lo, starts_ref[e, b + 1], lax.shift_right_logical(lo, ROW_CHUNK_LOG2)

    def needed(sp, kk):
        lo, hi, c0 = sp
        c = c0 + kk
        return (hi > lo) & (lax.shift_left(c, ROW_CHUNK_LOG2) < hi) & (c < n_chunks)

    def chunk_copy(e, sp, kk, slot):
        c = jnp.minimum(sp[2] + kk, n_chunks - 1)
        rows = pl.ds(pl.multiple_of(lax.shift_left(c, ROW_CHUNK_LOG2), ROW_CHUNK), ROW_CHUNK)
        return pltpu.make_async_copy(ye_ref.at[e, rows, :], buf_ref.at[slot, kk],
                                     sem_ref.at[slot, kk])

    def start_expert(e, slot):
        sp = span(e)
        for kk in range(n_cand):
            @pl.when(needed(sp, kk))
            def _():
                chunk_copy(e, sp, kk, slot).start()

    def wait_expert(e, sp, slot):
        for kk in range(n_cand):
            @pl.when(needed(sp, kk))
            def _():
                chunk_copy(e, sp, kk, slot).wait()

    acc_ref[...] = jnp.zeros_like(acc_ref)
    start_expert(0, 0)
    lane = lax.broadcasted_iota(I32, pos_ref.shape, 1)
    row_slot = lax.broadcasted_iota(I32, (t, ROW_CHUNK), 1)

    def expert_step(e, carry):
        slot = jnp.bitwise_and(e, 1)

        @pl.when(e + 1 < n_exp)
        def _():
            start_expert(e + 1, 1 - slot)

        sp = span(e)
        wait_expert(e, sp, slot)
        pos = jnp.sum(jnp.where(lane == e, pos_ref[...], 0), axis=1, keepdims=True)
        wgt = jnp.sum(jnp.where(lane == e, w_ref[...], 0.0), axis=1, keepdims=True)
        for kk in range(n_cand):
            @pl.when(needed(sp, kk))
            def _():
                base = lax.shift_left(sp[2] + kk, ROW_CHUNK_LOG2)
                onehot = (pos - base == row_slot).astype(BF16)
                acc_ref[...] += wgt * jnp.dot(onehot, buf_ref[slot, kk],
                                              preferred_element_type=F32)
        return carry

    lax.fori_loop(0, n_exp, expert_step, 0)
    xo_ref[...] = x_ref[...] + g2_ref[...] * acc_ref[...]


def _combine_call(starts, pos_tok, w_tok, x, gate2, ye):
    s, d = x.shape
    n_exp, cap, _ = ye.shape
    t = min(ROUTE_BLOCK, s)
    n_chunks = cap // ROW_CHUNK
    n_cand = min(t // ROW_CHUNK + 1, n_chunks)
    return pl.pallas_call(
        functools.partial(_combine_kernel, n_chunks, n_cand),
        grid_spec=pltpu.PrefetchScalarGridSpec(
            num_scalar_prefetch=1,
            grid=(s // t,),
            in_specs=[pl.BlockSpec((t, n_exp), lambda b, st: (b, 0)),
                      pl.BlockSpec((t, n_exp), lambda b, st: (b, 0)),
                      pl.BlockSpec((t, d), lambda b, st: (b, 0)),
                      pl.BlockSpec((1, d), lambda b, st: (0, 0)),
                      pl.BlockSpec(memory_space=pl.ANY)],
            out_specs=pl.BlockSpec((t, d), lambda b, st: (b, 0)),
            scratch_shapes=[pltpu.VMEM((t, d), F32),
                            pltpu.VMEM((2, n_cand, ROW_CHUNK, d), BF16),
                            pltpu.SemaphoreType.DMA((2, n_cand))],
        ),
        out_shape=jax.ShapeDtypeStruct((s, d), F32),
        compiler_params=_cparams("arbitrary"),
        name="expert_combine",
    )(starts, pos_tok, w_tok, x, gate2, ye)


def _rope_tables(seq_len):
    rows = seq_len // GRID_W
    row_id = jnp.repeat(jnp.arange(rows, dtype=F32), GRID_W)
    col_id = jnp.tile(jnp.arange(GRID_W, dtype=F32), rows)
    inv_freq = ROPE_THETA ** (-jnp.arange(0, ROPE_AXIS_DIM, 2, dtype=F32) / ROPE_AXIS_DIM)
    ang = jnp.stack([row_id[:, None] * inv_freq, col_id[:, None] * inv_freq], axis=1)
    cos, sin = jnp.cos(ang), jnp.sin(ang)
    cos_h = jnp.concatenate([cos, cos], axis=-1).reshape(seq_len, ATT_HEAD_DIM)
    sin_h = jnp.concatenate([-sin, sin], axis=-1).reshape(seq_len, ATT_HEAD_DIM)
    reps = LANES // ATT_HEAD_DIM
    return jnp.tile(cos_h, (1, reps)), jnp.tile(sin_h, (1, reps))


def _head_mean_operator():
    ii = np.arange(ATT_WIDTH)
    same = (ii[:, None] // ATT_HEAD_DIM) == (ii[None, :] // ATT_HEAD_DIM)
    return jnp.asarray(same.astype(np.float32) / ATT_HEAD_DIM, dtype=BF16)


def kernel(x, c, ada_w, ada_b, norm_mix_w, norm_ffn_w, w_in, q_norm_w, k_norm_w, hg_lower_bounds,
           hg_norm_w, w_branch_att, w_branch_hg, w_out, w_router, w_exp_gate, w_exp_up, w_exp_down):
    batch, s, d = x.shape
    assert batch == 1, "kernel is written for a single sequence"
    n_layers = ada_w.shape[0]
    cap = EC_CAPACITY_FACTOR * s // N_EXPERTS
    assert cap % ROW_CHUNK == 0 and s % min(ROUTE_BLOCK, s) == 0

    cos, sin = _rope_tables(s)
    bd = _head_mean_operator()
    mods = _ada_mod(c, ada_w, ada_b)
    qkv_w = ATT_WIDTH + 2 * ATT_KV_WIDTH
    hg_w = 5 * HG_WIDTH
    xs = x[0]
    for l in range(n_layers):
        shift1, scale1, gate1, shift2, scale2, gate2 = [
            mods[l, i * d:(i + 1) * d].reshape(1, d) for i in range(6)]
        qk_w = ATT_WIDTH + ATT_KV_WIDTH
        h, q, k, v = _qkv_call(
            xs, shift1, scale1, norm_mix_w[l].reshape(1, d), w_in[l, :, :qk_w].astype(BF16),
            w_in[l, :, qk_w:qkv_w].T.astype(BF16),
            jnp.tile(q_norm_w[l], ATT_HEADS).reshape(1, ATT_WIDTH),
            jnp.tile(k_norm_w[l], ATT_KV_HEADS).reshape(1, ATT_KV_WIDTH), cos, sin, bd)
        hq, kf, gf, kb, gb, hv, og = _hg_proj_call(
            l, h, w_in[l, :, qkv_w:qkv_w + hg_w].astype(BF16), hg_lower_bounds)
        g = _gate_call(h, w_in[l, :, qkv_w + hg_w:].astype(BF16))
        o_att = _attn_call(q, k, v)
        o_hg = _hg_call(hq, kf, gf, kb, gb, hv, og, hg_norm_w[l].reshape(1, HG_DIM))
        xs, h2, aff_t = _merge_call(
            xs, o_att, o_hg, g, w_branch_att[l].astype(BF16), w_branch_hg[l].astype(BF16),
            w_out[l].astype(BF16), gate1, shift2, scale2, norm_ffn_w[l].reshape(1, d),
            w_router[l].T)
        pos_t, starts = _select_call(aff_t, cap)
        xe = _gather_call(starts, pos_t, h2, cap)
        ye = _ffn_call(l, xe, w_exp_gate, w_exp_up, w_exp_down)
        xs = _combine_call(starts, pos_t.T, aff_t.T, xs, gate2, ye)
    return xs[None]
```

```python
import functools

import numpy as np
import jax
import jax.numpy as jnp
from jax import lax
from jax.experimental import pallas as pl
from jax.experimental.pallas import tpu as pltpu

F32 = jnp.float32
BF16 = jnp.bfloat16
I32 = jnp.int32
HIGHEST = lax.Precision.HIGHEST

GRID_W = 64
ATT_HEADS = 8
ATT_KV_HEADS = 2
ATT_GROUP = ATT_HEADS // ATT_KV_HEADS
ATT_HEAD_DIM = 64
ATT_WIDTH = ATT_HEADS * ATT_HEAD_DIM
ATT_KV_WIDTH = ATT_KV_HEADS * ATT_HEAD_DIM
ROPE_AXIS_DIM = ATT_HEAD_DIM // 2
ROPE_THETA = 10000.0
HG_HEADS = 4
HG_DIM = 128
HG_WIDTH = HG_HEADS * HG_DIM
MIN_FORGET = 1e-6
N_EXPERTS = 16
EC_CAPACITY_FACTOR = 2
NORM_EPS = 1e-6
LOG2_E = 1.4426950408889634

LANES = 128
VMEM_LIMIT_BYTES = 56 * 1024 * 1024

TOKEN_BLOCK = 512
ATT_Q_BLOCK = 512
ATT_KV_BLOCK = 2048
ATT_ROW_PARTS = 4
ATT_ONES_ROWS = 16
ATT_MAX_LAG = 64.0
ATT_INIT_KEYS = 256
HG_CHUNK = 128
HG_SUB = 16
ROUTE_BLOCK = 512
GATHER_BLOCK = 1024
ROW_CHUNK_LOG2 = 8
ROW_CHUNK = 1 << ROW_CHUNK_LOG2
FF_BLOCK = 512
FF_ROW_PARTS = 2


def _cparams(*sem):
    return pltpu.CompilerParams(dimension_semantics=sem, vmem_limit_bytes=VMEM_LIMIT_BYTES)


def _sigmoid(x):
    return 1.0 / (1.0 + jnp.exp(-x))


def _silu(x):
    return x * _sigmoid(x)


def _full(shape):
    return pl.BlockSpec(shape, lambda *_: (0,) * len(shape))


def _ada_kernel(c_ref, w_ref, b_ref, o_ref):
    ca = _silu(c_ref[...])
    o_ref[0] = jnp.dot(ca, w_ref[0], precision=HIGHEST, preferred_element_type=F32) + b_ref[0]


def _ada_mod(c, ada_w, ada_b):
    n_layers, d, w6 = ada_w.shape
    col = 1536
    c8 = jnp.broadcast_to(c, (8, d))
    out = pl.pallas_call(
        _ada_kernel,
        grid=(n_layers, w6 // col),
        in_specs=[
            pl.BlockSpec((8, d), lambda l, j: (0, 0)),
            pl.BlockSpec((1, d, col), lambda l, j: (l, 0, j)),
            pl.BlockSpec((1, 1, col), lambda l, j: (l, 0, j)),
        ],
        out_specs=pl.BlockSpec((1, 8, col), lambda l, j: (l, 0, j)),
        out_shape=jax.ShapeDtypeStruct((n_layers, 8, w6), F32),
        compiler_params=_cparams("parallel", "parallel"),
        name="ada_mod",
    )(c8, ada_w, ada_b.reshape(n_layers, 1, w6))
    return out[:, 0, :]


def _rms_mod(x, nw, scale, shift):
    ms = jnp.mean(x * x, axis=-1, keepdims=True)
    return (x * lax.rsqrt(ms + NORM_EPS) * nw) * (1.0 + scale) + shift


def _qkv_kernel(x_ref, sh_ref, sc_ref, nw_ref, w_ref, wvt_ref, qnw_ref, knw_ref, cos_ref,
                sin_ref, bd_ref, h_ref, q_ref, k_ref, vt_ref):
    h = _rms_mod(x_ref[...], nw_ref[...], sc_ref[...], sh_ref[...])
    hb = h.astype(BF16)
    h_ref[...] = hb
    proj = jnp.dot(hb, w_ref[...], preferred_element_type=F32)
    aq = proj[:, :ATT_WIDTH]
    ak = proj[:, ATT_WIDTH:]
    avt = lax.dot_general(wvt_ref[...], hb, (((1,), (1,)), ((), ())),
                          preferred_element_type=F32)
    bd = bd_ref[...]

    def head_mean(sq, op):
        hi = sq.astype(BF16)
        lo = (sq - hi.astype(F32)).astype(BF16)
        return (jnp.dot(hi, op, preferred_element_type=F32)
                + jnp.dot(lo, op, preferred_element_type=F32))

    qms = head_mean(aq * aq, bd)
    kms = head_mean(ak * ak, bd[:ATT_KV_WIDTH, :ATT_KV_WIDTH])
    qn = aq * lax.rsqrt(qms + NORM_EPS) * qnw_ref[...]
    kn = ak * lax.rsqrt(kms + NORM_EPS) * knw_ref[...]
    cos = cos_ref[...]
    sin = sin_ref[...]
    lane = lax.broadcasted_iota(I32, cos.shape, 1)
    first_half = (lane % ROPE_AXIS_DIM) < (ROPE_AXIS_DIM // 2)

    def rope(xs):
        up = pltpu.roll(xs, LANES - ROPE_AXIS_DIM // 2, 1)
        dn = pltpu.roll(xs, ROPE_AXIS_DIM // 2, 1)
        return xs * cos + jnp.where(first_half, up, dn) * sin

    q_scale = ATT_HEAD_DIM ** -0.5 * LOG2_E
    for j in range(ATT_WIDTH // LANES):
        qr = rope(qn[:, j * LANES:(j + 1) * LANES]) * q_scale
        q_ref[2 * j] = qr[:, :ATT_HEAD_DIM].astype(BF16)
        q_ref[2 * j + 1] = qr[:, ATT_HEAD_DIM:].astype(BF16)
    kr = rope(kn)
    ones = jnp.ones((ATT_ONES_ROWS, avt.shape[1]), F32)
    for j in range(ATT_KV_HEADS):
        k_ref[j] = kr[:, j * ATT_HEAD_DIM:(j + 1) * ATT_HEAD_DIM].astype(BF16)
        vj = avt[j * ATT_HEAD_DIM:(j + 1) * ATT_HEAD_DIM]
        vt_ref[j] = jnp.concatenate([vj, ones], axis=0).astype(BF16)


def _qkv_call(x, shift, scale, nw, w_qk, w_v_t, qnw, knw, cos, sin, bd):
    s, d = x.shape
    t = min(TOKEN_BLOCK, s)
    vr = ATT_HEAD_DIM + ATT_ONES_ROWS
    row = lambda i: (i, 0)
    head = lambda i: (0, i, 0)
    return pl.pallas_call(
        _qkv_kernel,
        grid=(s // t,),
        in_specs=[
            pl.BlockSpec((t, d), row), _full((1, d)), _full((1, d)), _full((1, d)),
            _full(w_qk.shape), _full(w_v_t.shape), _full((1, ATT_WIDTH)), _full((1, ATT_KV_WIDTH)),
            pl.BlockSpec((t, LANES), row), pl.BlockSpec((t, LANES), row), _full(bd.shape),
        ],
        out_specs=[
            pl.BlockSpec((t, d), row),
            pl.BlockSpec((ATT_HEADS, t, ATT_HEAD_DIM), head),
            pl.BlockSpec((ATT_KV_HEADS, t, ATT_HEAD_DIM), head),
            pl.BlockSpec((ATT_KV_HEADS, vr, t), lambda i: (0, 0, i)),
        ],
        out_shape=[
            jax.ShapeDtypeStruct((s, d), BF16),
            jax.ShapeDtypeStruct((ATT_HEADS, s, ATT_HEAD_DIM), BF16),
            jax.ShapeDtypeStruct((ATT_KV_HEADS, s, ATT_HEAD_DIM), BF16),
            jax.ShapeDtypeStruct((ATT_KV_HEADS, vr, s), BF16),
        ],
        compiler_params=_cparams("parallel"),
        name="qkv_proj",
    )(x, shift, scale, nw, w_qk, w_v_t, qnw, knw, cos, sin, bd)


def _hg_proj_kernel(layer, h_ref, w_ref, lb_ref, q_ref, kf_ref, gf_ref, kb_ref, gb_ref,
                    v_ref, og_ref):
    hb = h_ref[...]

    def proj(j):
        return jnp.dot(hb, w_ref[:, j * HG_WIDTH:(j + 1) * HG_WIDTH], preferred_element_type=F32)

    lb_raw = lb_ref[...]
    n_layers = lb_raw.shape[0]
    mx = lb_raw[0]
    for i in range(1, n_layers):
        mx = jnp.maximum(mx, lb_raw[i])
    ex = [jnp.exp(lb_raw[i] - mx) for i in range(n_layers)]
    den = ex[0]
    for i in range(1, n_layers):
        den = den + ex[i]
    lb = jnp.zeros_like(mx)
    for i in range(1, layer + 1):
        lb = lb + ex[i] / den

    def forget(z, lbd):
        f = lbd + (1.0 - lbd) * _sigmoid(z)
        return jnp.log(jnp.maximum(f, MIN_FORGET)), (1.0 - lbd) * _sigmoid(-z)

    q_ref[...] = _silu(proj(0))
    gf, kf = forget(proj(1), lb[0:1])
    gf_ref[...] = gf
    kf_ref[...] = kf
    gb, kb = forget(proj(2), lb[1:2])
    gb_ref[...] = gb
    kb_ref[...] = kb
    v_ref[...] = proj(3)
    og_ref[...] = _silu(proj(4))


def _hg_proj_call(layer, h, w_hg, lb_raw):
    s, d = h.shape
    t = min(TOKEN_BLOCK, s)
    row = lambda i: (i, 0)
    out = jax.ShapeDtypeStruct((s, HG_WIDTH), F32)
    return pl.pallas_call(
        functools.partial(_hg_proj_kernel, layer),
        grid=(s // t,),
        in_specs=[pl.BlockSpec((t, d), row), _full(w_hg.shape), _full(lb_raw.shape)],
        out_specs=[pl.BlockSpec((t, HG_WIDTH), row)] * 7,
        out_shape=[out] * 7,
        compiler_params=_cparams("parallel"),
        name="hg_proj",
    )(h, w_hg, lb_raw)


def _gate_kernel(h_ref, w_ref, g_ref):
    g_ref[...] = _sigmoid(jnp.dot(h_ref[...], w_ref[...], preferred_element_type=F32)).astype(BF16)


def _gate_call(h, w_gate):
    s, d = h.shape
    t = min(TOKEN_BLOCK, s)
    n = w_gate.shape[1]
    return pl.pallas_call(
        _gate_kernel,
        grid=(s // t,),
        in_specs=[pl.BlockSpec((t, d), lambda i: (i, 0)), _full(w_gate.shape)],
        out_specs=pl.BlockSpec((t, n), lambda i: (i, 0)),
        out_shape=jax.ShapeDtypeStruct((s, n), BF16),
        compiler_params=_cparams("parallel"),
        name="gate_proj",
    )(h, w_gate)


def _attn_kernel(q_ref, k_ref, vt_ref, o_ref, m_ref, acc_ref):
    g, tq, hd = q_ref.shape
    s = k_ref.shape[1]
    tk = min(ATT_KV_BLOCK, s)
    heads_per_part = g // ATT_ROW_PARTS
    cols = heads_per_part * tq
    acc_ref[...] = jnp.zeros(acc_ref.shape, F32)

    def scores(ks, r):
        q = q_ref[r * heads_per_part:(r + 1) * heads_per_part].reshape(cols, hd)
        return lax.dot_general(ks, q, (((1,), (1,)), ((), ())), preferred_element_type=F32)

    ks0 = k_ref[0, pl.ds(0, min(ATT_INIT_KEYS, tk)), :]
    for r in range(ATT_ROW_PARTS):
        m_ref[:, pl.ds(r * cols, cols)] = jnp.max(scores(ks0, r), axis=0, keepdims=True)

    def body(j, carry):
        start = pl.multiple_of(j * tk, tk)
        ks = k_ref[0, pl.ds(start, tk), :]
        vt = vt_ref[0, :, pl.ds(start, tk)]
        parts = [pl.ds(r * cols, cols) for r in range(ATT_ROW_PARTS)]
        sts = [scores(ks, r) for r in range(ATT_ROW_PARTS)]
        pvs, m_news, lag = [], [], []
        for sl, st in zip(parts, sts):
            m_old = m_ref[:, sl]
            m_blk = jnp.max(st, axis=0, keepdims=True)
            p = jnp.exp2(st - m_old).astype(BF16)
            pvs.append(jnp.dot(vt, p, preferred_element_type=F32))
            m_news.append(jnp.maximum(m_old, m_blk))
            lag.append(jnp.max(m_blk - m_old))
        worst_lag = functools.reduce(jnp.maximum, lag)
        safe = worst_lag <= ATT_MAX_LAG

        @pl.when(safe)
        def _():
            for sl, pv, m_new in zip(parts, pvs, m_news):
                acc_ref[:, sl] = (acc_ref[:, sl] + pv) * jnp.exp2(m_ref[:, sl] - m_new)
                m_ref[:, sl] = m_new

        @pl.when(jnp.logical_not(safe))
        def _():
            for r, sl in enumerate(parts):
                st = scores(ks, r)
                m_old = m_ref[:, sl]
                m_new = jnp.maximum(m_old, jnp.max(st, axis=0, keepdims=True))
                p = jnp.exp2(st - m_new).astype(BF16)
                acc_ref[:, sl] = (jnp.exp2(m_old - m_new) * acc_ref[:, sl]
                                  + jnp.dot(vt, p, preferred_element_type=F32))
                m_ref[:, sl] = m_new
        return carry

    lax.fori_loop(0, s // tk, body, 0)
    acc = acc_ref[...]
    o_t = acc[:hd] / acc[hd:hd + 1]
    for i in range(g):
        o_ref[:, i * hd:(i + 1) * hd] = o_t[:, i * tq:(i + 1) * tq].T.astype(BF16)


def _attn_call(q, k, vt):
    _, s, hd = q.shape
    vr = vt.shape[1]
    tq = min(ATT_Q_BLOCK, s)
    return pl.pallas_call(
        _attn_kernel,
        grid=(ATT_KV_HEADS, s // tq),
        in_specs=[
            pl.BlockSpec((ATT_GROUP, tq, hd), lambda h, i: (h, i, 0)),
            pl.BlockSpec((1, s, hd), lambda h, i: (h, 0, 0)),
            pl.BlockSpec((1, vr, s), lambda h, i: (h, 0, 0)),
        ],
        out_specs=pl.BlockSpec((tq, ATT_GROUP * hd), lambda h, i: (i, h)),
        out_shape=jax.ShapeDtypeStruct((s, ATT_HEADS * hd), BF16),
        scratch_shapes=[pltpu.VMEM((1, ATT_GROUP * tq), F32),
                        pltpu.VMEM((vr, ATT_GROUP * tq), F32)],
        compiler_params=_cparams("parallel", "parallel"),
        name="attention",
    )(q, k, vt)


def _chunk_cumsum(g, reverse):
    c = g.shape[0]
    row = lax.broadcasted_iota(I32, g.shape, 0)
    x = g
    step = 1
    while step < c:
        if not reverse:
            x = x + jnp.where(row >= step, pltpu.roll(x, step, 0), 0.0)
        else:
            x = x + jnp.where(row < c - step, pltpu.roll(x, c - step, 0), 0.0)
        step *= 2
    return x


def _boundary_rows(b_ref, half, idx):
    c, dd = b_ref.shape
    blk = 2 * half
    pieces = []
    if blk >= 8:
        for s0 in range(0, c, blk):
            pieces.append(jnp.broadcast_to(b_ref[pl.ds(s0 + idx, 1), :], (blk, dd)))
    else:
        sub = lax.broadcasted_iota(I32, (8, dd), 0)
        for v0 in range(0, c, 8):
            piece = jnp.broadcast_to(b_ref[pl.ds(v0 + idx, 1), :], (8, dd))
            for s0 in range(v0 + blk, v0 + 8, blk):
                row = jnp.broadcast_to(b_ref[pl.ds(s0 + idx, 1), :], (8, dd))
                piece = jnp.where(sub >= s0 - v0, row, piece)
            pieces.append(piece)
    return jnp.concatenate(pieces, axis=0)


def _hg_scan_body(reverse, q_ref, k_ref, v_ref, g_ref, mask_ref, st_ref, b_ref):
    c = q_ref.shape[0]
    heads = [slice(h * HG_DIM, (h + 1) * HG_DIM) for h in range(q_ref.shape[1] // HG_DIM)]
    nt = (((1,), (1,)), ((), ()))
    q = q_ref[...]
    k = k_ref[...]
    vb = v_ref[...].astype(BF16)
    b = _chunk_cumsum(g_ref[...] * LOG2_E, reverse)
    b_ref[...] = b
    b_edge = b_ref[pl.ds(c - 1 if not reverse else 0, 1), :]
    qh = (q * jnp.exp2(b)).astype(BF16)
    kh = (k * jnp.exp2(b_edge - b)).astype(BF16)
    st_decay = jnp.exp2(b_edge)
    outs = []
    for h, sl in enumerate(heads):
        st = st_ref[h]
        outs.append(lax.dot_general(qh[:, sl], st.astype(BF16), nt, preferred_element_type=F32))
        st_ref[h] = st * st_decay[:, sl] + lax.dot_general(
            vb[:, sl], kh[:, sl], (((0,), (0,)), ((), ())), preferred_element_type=F32)

    qb = q.astype(BF16)
    kb = k.astype(BF16)
    scores = [mask_ref[0] * lax.dot_general(qb[:, sl], kb[:, sl], nt, preferred_element_type=F32)
              for sl in heads]
    half = c // 2
    level = 1
    while half >= 1:
        d = b - _boundary_rows(b_ref, half, half - 1 if not reverse else half)
        t = jnp.exp2(-jnp.abs(d))
        qt = (q * t).astype(BF16)
        kt = (k * t).astype(BF16)
        mask = mask_ref[level]
        scores = [a + mask * lax.dot_general(qt[:, sl], kt[:, sl], nt, preferred_element_type=F32)
                  for a, sl in zip(scores, heads)]
        half //= 2
        level += 1
    return [o + jnp.dot(a.astype(BF16), vb[:, sl], preferred_element_type=F32)
            for o, a, sl in zip(outs, scores, heads)]


def _hg_fwd_kernel(q_ref, k_ref, v_ref, g_ref, mask_ref, o_ref, st_ref, b_ref):
    @pl.when(pl.program_id(0) == 0)
    def _():
        st_ref[...] = jnp.zeros_like(st_ref)

    outs = _hg_scan_body(False, q_ref, k_ref, v_ref, g_ref, mask_ref, st_ref, b_ref)
    o_ref[...] = jnp.concatenate(outs, axis=1)


def _hg_bwd_kernel(q_ref, k_ref, v_ref, g_ref, mask_ref, of_ref, og_ref, nw_ref, o_ref, st_ref,
                   b_ref):
    @pl.when(pl.program_id(0) == 0)
    def _():
        st_ref[...] = jnp.zeros_like(st_ref)

    outs = _hg_scan_body(True, q_ref, k_ref, v_ref, g_ref, mask_ref, st_ref, b_ref)
    normed = []
    for h, o_b in enumerate(outs):
        o = of_ref[:, h * HG_DIM:(h + 1) * HG_DIM] + o_b
        ms = jnp.mean(o * o, axis=-1, keepdims=True)
        normed.append(o * lax.rsqrt(ms + NORM_EPS) * nw_ref[...])
    o_ref[...] = (jnp.concatenate(normed, axis=1) * og_ref[...]).astype(BF16)


def _level_masks(c, reverse):
    ii = np.arange(c)[:, None]
    jj = np.arange(c)[None, :]
    masks = [ii == jj]
    half = c // 2
    while half >= 1:
        blk = 2 * half
        same = (ii // blk) == (jj // blk)
        q_hi, k_hi = (ii % blk) >= half, (jj % blk) >= half
        masks.append(same & (q_hi & ~k_hi if not reverse else ~q_hi & k_hi))
        half //= 2
    return jnp.asarray(np.stack(masks).astype(np.float32))


def _hg_call(q, kf, gf, kb, gb, v, og, nw):
    s = q.shape[0]
    c = min(HG_CHUNK, s)
    n = s // c
    mask_f = _level_masks(c, False)
    mask_b = _level_masks(c, True)
    fwd = lambda i: (i, 0)
    bwd = lambda i: (n - 1 - i, 0)
    blk = lambda m: pl.BlockSpec((c, HG_WIDTH), m)
    scratch = [pltpu.VMEM((HG_HEADS, HG_DIM, HG_DIM), F32), pltpu.VMEM((c, HG_WIDTH), F32)]
    o_fwd = pl.pallas_call(
        _hg_fwd_kernel,
        grid=(n,),
        in_specs=[blk(fwd)] * 4 + [_full(mask_f.shape)],
        out_specs=blk(fwd),
        out_shape=jax.ShapeDtypeStruct((s, HG_WIDTH), F32),
        scratch_shapes=scratch,
        compiler_params=_cparams("arbitrary"),
        name="hg_scan_fwd",
    )(q, kf, v, gf, mask_f)
    return pl.pallas_call(
        _hg_bwd_kernel,
        grid=(n,),
        in_specs=[blk(bwd)] * 4 + [_full(mask_b.shape), blk(bwd), blk(bwd), _full((1, HG_DIM))],
        out_specs=blk(bwd),
        out_shape=jax.ShapeDtypeStruct((s, HG_WIDTH), BF16),
        scratch_shapes=scratch,
        compiler_params=_cparams("arbitrary"),
        name="hg_scan_bwd",
    )(q, kb, v, gb, mask_b, o_fwd, og, nw)


def _merge_kernel(x_ref, oa_ref, oh_ref, hm_ref, wg_ref, wa_ref, wh_ref, wo_ref, g1_ref, sh_ref,
                  sc_ref, nw_ref, wr_ref, xo_ref, h_ref, aff_ref):
    d = x_ref.shape[1]
    gates = _sigmoid(jnp.dot(hm_ref[...], wg_ref[...], preferred_element_type=F32))
    ya = jnp.dot(oa_ref[...], wa_ref[...], preferred_element_type=F32)
    yh = jnp.dot(oh_ref[...], wh_ref[...], preferred_element_type=F32)
    merged = gates[:, :d] * ya + gates[:, d:] * yh
    y = jnp.dot(merged.astype(BF16), wo_ref[...], preferred_element_type=F32)
    xn = x_ref[...] + g1_ref[...] * y
    xo_ref[...] = xn
    h = _rms_mod(xn, nw_ref[...], sc_ref[...], sh_ref[...])
    h_ref[...] = h.astype(BF16)
    logits = lax.dot_general(wr_ref[...], h, (((1,), (1,)), ((), ())), precision=HIGHEST,
                             preferred_element_type=F32)
    ex = jnp.exp(logits - jnp.max(logits, axis=0, keepdims=True))
    aff_ref[...] = ex / jnp.sum(ex, axis=0, keepdims=True)


def _merge_call(x, o_att, o_hg, h_mix, w_gate, w_ba, w_bh, w_out, gate1, shift2, scale2, nw2,
                w_router_t):
    s, d = x.shape
    t = min(TOKEN_BLOCK, s)
    row = lambda i: (i, 0)
    return pl.pallas_call(
        _merge_kernel,
        grid=(s // t,),
        in_specs=[
            pl.BlockSpec((t, d), row), pl.BlockSpec((t, ATT_WIDTH), row),
            pl.BlockSpec((t, HG_WIDTH), row), pl.BlockSpec((t, d), row), _full(w_gate.shape),
            _full(w_ba.shape), _full(w_bh.shape), _full(w_out.shape),
            _full((1, d)), _full((1, d)), _full((1, d)), _full((1, d)), _full(w_router_t.shape),
        ],
        out_specs=[pl.BlockSpec((t, d), row), pl.BlockSpec((t, d), row),
                   pl.BlockSpec((N_EXPERTS, t), lambda i: (0, i))],
        out_shape=[jax.ShapeDtypeStruct((s, d), F32), jax.ShapeDtypeStruct((s, d), BF16),
                   jax.ShapeDtypeStruct((N_EXPERTS, s), F32)],
        compiler_params=_cparams("parallel"),
        name="merge_router",
    )(x, o_att, o_hg, h_mix, w_gate, w_ba, w_bh, w_out, gate1, shift2, scale2, nw2, w_router_t)


def _select_kernel(cap, aff_ref, tri_ref, pos_ref, starts_ref):
    n_exp, s = aff_ref.shape
    t = tri_ref.shape[0]
    n_blk = s // t
    aff = aff_ref[...]

    def bit_step(i, thr_bits):
        cand = thr_bits | lax.shift_left(jnp.int32(1), jnp.int32(30) - i)
        cnt = jnp.sum((aff >= pltpu.bitcast(cand, F32)).astype(F32), axis=1, keepdims=True)
        return jnp.where(cnt >= cap, cand, thr_bits)

    thr = pltpu.bitcast(lax.fori_loop(0, 31, bit_step, jnp.zeros((n_exp, 1), I32)), F32)
    n_gt = jnp.sum((aff > thr).astype(F32), axis=1, keepdims=True)
    need_eq = cap - n_gt
    tri = tri_ref[...]
    blk_lane = lax.broadcasted_iota(I32, starts_ref.shape, 1)

    starts_ref[...] = jnp.zeros(starts_ref.shape, I32)

    def blk_step(j, carry):
        c_eq, c_sel = carry
        start = pl.multiple_of(j * t, t)
        bt = aff_ref[:, pl.ds(start, t)]
        eq = bt == thr
        eq_f = eq.astype(BF16)
        eq_excl = c_eq + jnp.dot(eq_f, tri, preferred_element_type=F32) - eq_f.astype(F32)
        sel = (bt > thr) | (eq & (eq_excl < need_eq))
        sel_f = sel.astype(BF16)
        sel_incl = c_sel + jnp.dot(sel_f, tri, preferred_element_type=F32)
        pos_ref[:, pl.ds(start, t)] = jnp.where(sel, sel_incl - 1.0, -1.0).astype(I32)
        starts_ref[...] = jnp.where(blk_lane == j, c_sel.astype(I32), starts_ref[...])
        return (c_eq + jnp.sum(eq_f.astype(F32), axis=1, keepdims=True),
                c_sel + jnp.sum(sel_f.astype(F32), axis=1, keepdims=True))

    zero = jnp.zeros((n_exp, 1), F32)
    _, c_sel = lax.fori_loop(0, n_blk, blk_step, (zero, zero))
    starts_ref[...] = jnp.where(blk_lane >= n_blk, c_sel.astype(I32), starts_ref[...])


def _select_call(aff_t, cap):
    n_exp, s = aff_t.shape
    t = min(ROUTE_BLOCK, s)
    ii = np.arange(t)
    tri = jnp.asarray((ii[:, None] <= ii[None, :]).astype(np.float32), dtype=BF16)
    return pl.pallas_call(
        functools.partial(_select_kernel, float(cap)),
        grid=(1,),
        in_specs=[_full((n_exp, s)), _full((t, t))],
        out_specs=[_full((n_exp, s)), _full((n_exp, LANES))],
        out_shape=[jax.ShapeDtypeStruct((n_exp, s), I32), jax.ShapeDtypeStruct((n_exp, LANES), I32)],
        compiler_params=_cparams("arbitrary"),
        name="route_select",
    )(aff_t, tri)


def _gather_kernel(cap, per, starts_ref, pos_ref, h_ref, xe_ref, acc_ref):
    e = pl.program_id(0)
    b = pl.program_id(1)
    t = h_ref.shape[0]

    @pl.when(b == 0)
    def _():
        acc_ref[...] = jnp.zeros_like(acc_ref)

    lo = starts_ref[e, b * per]
    hi = starts_ref[e, (b + 1) * per]
    c0 = lax.shift_right_logical(lo, ROW_CHUNK_LOG2)
    pos = pos_ref[pl.ds(e, 1), :]
    slot = lax.broadcasted_iota(I32, (ROW_CHUNK, t), 0)
    for kk in range(t // ROW_CHUNK + 1):
        c = c0 + kk

        @pl.when((hi > lo) & (c * ROW_CHUNK < hi) & (c < cap // ROW_CHUNK))
        def _():
            base = pl.multiple_of(c * ROW_CHUNK, ROW_CHUNK)
            onehot = (pos - base == slot).astype(BF16)
            acc_ref[pl.ds(base, ROW_CHUNK), :] += jnp.dot(onehot, h_ref[...],
                                                          preferred_element_type=F32)

    @pl.when(b == pl.num_programs(1) - 1)
    def _():
        xe_ref[0] = acc_ref[...].astype(BF16)


def _gather_call(starts, pos_t, h, cap):
    s, d = h.shape
    t = min(GATHER_BLOCK, s)
    return pl.pallas_call(
        functools.partial(_gather_kernel, cap, t // min(ROUTE_BLOCK, s)),
        grid_spec=pltpu.PrefetchScalarGridSpec(
            num_scalar_prefetch=1,
            grid=(N_EXPERTS, s // t),
            in_specs=[pl.BlockSpec((N_EXPERTS, t), lambda e, b, st: (0, b)),
                      pl.BlockSpec((t, d), lambda e, b, st: (b, 0))],
            out_specs=pl.BlockSpec((1, cap, d), lambda e, b, st: (e, 0, 0)),
            scratch_shapes=[pltpu.VMEM((cap, d), F32)],
        ),
        out_shape=jax.ShapeDtypeStruct((N_EXPERTS, cap, d), BF16),
        compiler_params=_cparams("parallel", "arbitrary"),
        name="expert_gather",
    )(starts, pos_t, h)


def _ffn_kernel(xe_ref, wg_ref, wu_ref, wd_ref, ye_ref, acc_ref):
    f = pl.program_id(1)
    last = pl.num_programs(1) - 1
    rows = xe_ref.shape[1] // FF_ROW_PARTS
    wg = wg_ref[0, 0].astype(BF16)
    wu = wu_ref[0, 0].astype(BF16)
    wd = wd_ref[0, 0].astype(BF16)
    parts = [pl.ds(r * rows, rows) for r in range(FF_ROW_PARTS)]
    gate_up = []
    for sl in parts:
        x = xe_ref[0, sl, :]
        gate_up.append((jnp.dot(x, wg, preferred_element_type=F32),
                        jnp.dot(x, wu, preferred_element_type=F32)))
    hids = [(_silu(a) * u).astype(BF16) for a, u in gate_up]

    def down(hid):
        return jnp.dot(hid, wd, preferred_element_type=F32)

    @pl.when(f == 0)
    def _():
        for sl, hid in zip(parts, hids):
            acc_ref[sl, :] = down(hid)

    @pl.when((f > 0) & (f < last))
    def _():
        for sl, hid in zip(parts, hids):
            acc_ref[sl, :] += down(hid)

    @pl.when(f == last)
    def _():
        for sl, hid in zip(parts, hids):
            ye_ref[0, sl, :] = (acc_ref[sl, :] + down(hid)).astype(BF16)


def _ffn_call(layer, xe, w_gate, w_up, w_down):
    n_exp, cap, d = xe.shape
    ff = w_gate.shape[3]
    fb = min(FF_BLOCK, ff)
    assert ff // fb >= 2, "the first and last hidden blocks take different accumulate paths"
    return pl.pallas_call(
        _ffn_kernel,
        grid=(n_exp, ff // fb),
        in_specs=[pl.BlockSpec((1, cap, d), lambda e, f: (e, 0, 0)),
                  pl.BlockSpec((1, 1, d, fb), lambda e, f: (layer, e, 0, f)),
                  pl.BlockSpec((1, 1, d, fb), lambda e, f: (layer, e, 0, f)),
                  pl.BlockSpec((1, 1, fb, d), lambda e, f: (layer, e, f, 0))],
        out_specs=pl.BlockSpec((1, cap, d), lambda e, f: (e, 0, 0)),
        out_shape=jax.ShapeDtypeStruct((n_exp, cap, d), BF16),
        scratch_shapes=[pltpu.VMEM((cap, d), F32)],
        compiler_params=_cparams("parallel", "arbitrary"),
        name="expert_ffn",
    )(xe, w_gate, w_up, w_down)


def _combine_kernel(n_chunks, n_cand, starts_ref, pos_ref, w_ref, x_ref, g2_ref, ye_ref, xo_ref,
                    acc_ref, buf_ref, sem_ref):
    b = pl.program_id(0)
    t = x_ref.shape[0]
    n_exp = pos_ref.shape[1]

    def span(e):
        lo = starts_ref[e, b]
        return lo, starts_ref[e, b + 1], lax.shift_right_logical(lo, ROW_CHUNK_LOG2)

    def needed(sp, kk):
        lo, hi, c0 = sp
        c = c0 + kk
        return (hi > lo) & (lax.shift_left(c, ROW_CHUNK_LOG2) < hi) & (c < n_chunks)

    def chunk_copy(e, sp, kk, slot):
        c = jnp.minimum(sp[2] + kk, n_chunks - 1)
        rows = pl.ds(pl.multiple_of(lax.shift_left(c, ROW_CHUNK_LOG2), ROW_CHUNK), ROW_CHUNK)
        return pltpu.make_async_copy(ye_ref.at[e, rows, :], buf_ref.at[slot, kk],
                                     sem_ref.at[slot, kk])

    def start_expert(e, slot):
        sp = span(e)
        for kk in range(n_cand):
            @pl.when(needed(sp, kk))
            def _():
                chunk_copy(e, sp, kk, slot).start()

    def wait_expert(e, sp, slot):
        for kk in range(n_cand):
            @pl.when(needed(sp, kk))
            def _():
                chunk_copy(e, sp, kk, slot).wait()

    acc_ref[...] = jnp.zeros_like(acc_ref)
    start_expert(0, 0)
    lane = lax.broadcasted_iota(I32, pos_ref.shape, 1)
    row_slot = lax.broadcasted_iota(I32, (t, ROW_CHUNK), 1)

    def expert_step(e, carry):
        slot = jnp.bitwise_and(e, 1)

        @pl.when(e + 1 < n_exp)
        def _():
            start_expert(e + 1, 1 - slot)

        sp = span(e)
        wait_expert(e, sp, slot)
        pos = jnp.sum(jnp.where(lane == e, pos_ref[...], 0), axis=1, keepdims=True)
        wgt = jnp.sum(jnp.where(lane == e, w_ref[...], 0.0), axis=1, keepdims=True)
        for kk in range(n_cand):
            @pl.when(needed(sp, kk))
            def _():
                base = lax.shift_left(sp[2] + kk, ROW_CHUNK_LOG2)
                onehot = (pos - base == row_slot).astype(BF16)
                acc_ref[...] += wgt * jnp.dot(onehot, buf_ref[slot, kk],
                                              preferred_element_type=F32)
        return carry

    lax.fori_loop(0, n_exp, expert_step, 0)
    xo_ref[...] = x_ref[...] + g2_ref[...] * acc_ref[...]


def _combine_call(starts, pos_tok, w_tok, x, gate2, ye):
    s, d = x.shape
    n_exp, cap, _ = ye.shape
    t = min(ROUTE_BLOCK, s)
    n_chunks = cap // ROW_CHUNK
    n_cand = min(t // ROW_CHUNK + 1, n_chunks)
    return pl.pallas_call(
        functools.partial(_combine_kernel, n_chunks, n_cand),
        grid_spec=pltpu.PrefetchScalarGridSpec(
            num_scalar_prefetch=1,
            grid=(s // t,),
            in_specs=[pl.BlockSpec((t, n_exp), lambda b, st: (b, 0)),
                      pl.BlockSpec((t, n_exp), lambda b, st: (b, 0)),
                      pl.BlockSpec((t, d), lambda b, st: (b, 0)),
                      pl.BlockSpec((1, d), lambda b, st: (0, 0)),
                      pl.BlockSpec(memory_space=pl.ANY)],
            out_specs=pl.BlockSpec((t, d), lambda b, st: (b, 0)),
            scratch_shapes=[pltpu.VMEM((t, d), F32),
                            pltpu.VMEM((2, n_cand, ROW_CHUNK, d), BF16),
                            pltpu.SemaphoreType.DMA((2, n_cand))],
        ),
        out_shape=jax.ShapeDtypeStruct((s, d), F32),
        compiler_params=_cparams("arbitrary"),
        name="expert_combine",
    )(starts, pos_tok, w_tok, x, gate2, ye)


def _rope_tables(seq_len):
    rows = seq_len // GRID_W
    row_id = jnp.repeat(jnp.arange(rows, dtype=F32), GRID_W)
    col_id = jnp.tile(jnp.arange(GRID_W, dtype=F32), rows)
    inv_freq = ROPE_THETA ** (-jnp.arange(0, ROPE_AXIS_DIM, 2, dtype=F32) / ROPE_AXIS_DIM)
    ang = jnp.stack([row_id[:, None] * inv_freq, col_id[:, None] * inv_freq], axis=1)
    cos, sin = jnp.cos(ang), jnp.sin(ang)
    cos_h = jnp.concatenate([cos, cos], axis=-1).reshape(seq_len, ATT_HEAD_DIM)
    sin_h = jnp.concatenate([-sin, sin], axis=-1).reshape(seq_len, ATT_HEAD_DIM)
    reps = LANES // ATT_HEAD_DIM
    return jnp.tile(cos_h, (1, reps)), jnp.tile(sin_h, (1, reps))


def _head_mean_operator():
    ii = np.arange(ATT_WIDTH)
    same = (ii[:, None] // ATT_HEAD_DIM) == (ii[None, :] // ATT_HEAD_DIM)
    return jnp.asarray(same.astype(np.float32) / ATT_HEAD_DIM, dtype=BF16)


def kernel(x, c, ada_w, ada_b, norm_mix_w, norm_ffn_w, w_in, q_norm_w, k_norm_w, hg_lower_bounds,
           hg_norm_w, w_branch_att, w_branch_hg, w_out, w_router, w_exp_gate, w_exp_up, w_exp_down):
    batch, s, d = x.shape
    assert batch == 1, "kernel is written for a single sequence"
    n_layers = ada_w.shape[0]
    cap = EC_CAPACITY_FACTOR * s // N_EXPERTS
    assert cap % ROW_CHUNK == 0 and s % min(ROUTE_BLOCK, s) == 0

    cos, sin = _rope_tables(s)
    bd = _head_mean_operator()
    mods = _ada_mod(c, ada_w, ada_b)
    qkv_w = ATT_WIDTH + 2 * ATT_KV_WIDTH
    hg_w = 5 * HG_WIDTH
    xs = x[0]
    for l in range(n_layers):
        shift1, scale1, gate1, shift2, scale2, gate2 = [
            mods[l, i * d:(i + 1) * d].reshape(1, d) for i in range(6)]
        qk_w = ATT_WIDTH + ATT_KV_WIDTH
        h, q, k, v = _qkv_call(
            xs, shift1, scale1, norm_mix_w[l].reshape(1, d), w_in[l, :, :qk_w].astype(BF16),
            w_in[l, :, qk_w:qkv_w].T.astype(BF16),
            jnp.tile(q_norm_w[l], ATT_HEADS).reshape(1, ATT_WIDTH),
            jnp.tile(k_norm_w[l], ATT_KV_HEADS).reshape(1, ATT_KV_WIDTH), cos, sin, bd)
        hq, kf, gf, kb, gb, hv, og = _hg_proj_call(
            l, h, w_in[l, :, qkv_w:qkv_w + hg_w].astype(BF16), hg_lower_bounds)
        o_att = _attn_call(q, k, v)
        o_hg = _hg_call(hq, kf, gf, kb, gb, hv, og, hg_norm_w[l].reshape(1, HG_DIM))
        xs, h2, aff_t = _merge_call(
            xs, o_att, o_hg, h, w_in[l, :, qkv_w + hg_w:].astype(BF16),
            w_branch_att[l].astype(BF16), w_branch_hg[l].astype(BF16),
            w_out[l].astype(BF16), gate1, shift2, scale2, norm_ffn_w[l].reshape(1, d),
            w_router[l].T)
        pos_t, starts = _select_call(aff_t, cap)
        xe = _gather_call(starts, pos_t, h2, cap)
        ye = _ffn_call(l, xe, w_exp_gate, w_exp_up, w_exp_down)
        xs = _combine_call(starts, pos_t.T, aff_t.T, xs, gate2, ye)
    return xs[None]
```
